```python
import math
import jax, jax.numpy as jnp
from jax import lax
import numpy as np

D_MODEL = 4096
BATCH = 2
SEQ = 4096
DEPTH = 1
DEC_BATCH = 128
DEC_SEQ = 4
PAST_LEN = 2048
PAGE_SIZE = 128

W_A = D_MODEL // 2
HGRN_EXPAND = 128
HGRN_HEADS = W_A // HGRN_EXPAND
HGRN_DK = HGRN_EXPAND
HGRN_DV = W_A // HGRN_HEADS
HGRN_CHUNK = 64
HEAD_DIM = 128
DIL_GROUPS = ((128, 1), (512, 4), (2048, 16))
N_GROUPS = 3
H_G = D_MODEL // 512
W_B = N_GROUPS * H_G * HEAD_DIM
W_BO = H_G * HEAD_DIM
SPLIT_SIZES = (W_A, W_A, W_A, W_A, W_B, W_B, W_B, W_BO, D_MODEL, D_MODEL)
N_IN = 4 * W_A + 3 * W_B + W_BO + 2 * D_MODEL
RMS_EPS = 1e-6

kernel_name = 'hybrid_hgrn2_dilated_attn_decode_step'


def _rms_norm(x, w):
    xf = x.astype(jnp.float32)
    y = xf * lax.rsqrt(jnp.mean(xf * xf, axis=-1, keepdims=True) + RMS_EPS)
    return (y * w.astype(jnp.float32)).astype(x.dtype)


def _split_in(u):
    parts = []
    off = 0
    for n in SPLIT_SIZES:
        parts.append(u[..., off:off + n])
        off += n
    return parts


def _hgrn2_chunkwise(q, k, v, log_f, s0, chunk):
    B, T, H, DK = q.shape
    DV = v.shape[-1]
    n = T // chunk
    qc = q.reshape(B, n, chunk, H, DK)
    kc = k.reshape(B, n, chunk, H, DK)
    vc = v.reshape(B, n, chunk, H, DV)
    G = jnp.cumsum(log_f.reshape(B, n, chunk, H, DK), axis=2)
    G_last = G[:, :, -1:]
    q_dec = qc * jnp.exp(G)
    k_rel = kc * jnp.exp(-G)
    k_end = kc * jnp.exp(G_last - G)
    causal = jnp.tril(jnp.ones((chunk, chunk), dtype=bool))
    A = jnp.einsum('bnihd,bnjhd->bnhij', q_dec, k_rel)
    A = jnp.where(causal, A, 0.0)
    o_intra = jnp.einsum('bnhij,bnjhe->bnihe', A, vc)
    dS = jnp.einsum('bnjhd,bnjhe->bnhde', k_end, vc)
    decay = jnp.exp(G_last[:, :, 0])

    def step(S, inp):
        d, ds = inp
        return d[..., None] * S + ds, S

    s_fin, s_prev = lax.scan(step, s0, (jnp.swapaxes(decay, 0, 1), jnp.swapaxes(dS, 0, 1)))
    s_prev = jnp.swapaxes(s_prev, 0, 1)
    o_inter = jnp.einsum('bnihd,bnhde->bnihe', q_dec, s_prev)
    return (o_intra + o_inter).reshape(B, T, H, DV), s_fin


def _hgrn2_branch(qa, fa, ia, za, lb, s0, norm_w):
    B, T, _ = qa.shape
    chunk = math.gcd(T, HGRN_CHUNK)
    q = jax.nn.silu(qa.astype(jnp.float32)).reshape(B, T, HGRN_HEADS, HGRN_DK) * (HGRN_DK ** -0.5)
    lbf = lb.astype(jnp.float32)
    f = lbf + (1.0 - lbf) * jax.nn.sigmoid(fa.astype(jnp.float32))
    log_f = jnp.log(f).reshape(B, T, HGRN_HEADS, HGRN_DK)
    k = (1.0 - f).reshape(B, T, HGRN_HEADS, HGRN_DK)
    v = ia.astype(jnp.float32).reshape(B, T, HGRN_HEADS, HGRN_DV)
    o, s_fin = _hgrn2_chunkwise(q, k, v, log_f, s0.astype(jnp.float32), chunk)
    o = _rms_norm(o, norm_w).reshape(B, T, W_A)
    return o.astype(qa.dtype) * jax.nn.silu(za), s_fin


def _dilated_prompt(q, k, v, window, dil):
    B, S, H, Dh = q.shape
    wk = window // dil
    span = wk * dil
    s_pad = -(-S // span) * span
    nb = s_pad // span

    def blocks(a):
        a = jnp.pad(a, ((0, 0), (0, s_pad - S), (0, 0), (0, 0)))
        return a.reshape(B, nb, wk, dil, H, Dh)

    def with_prev(a):
        prev = jnp.concatenate([jnp.zeros_like(a[:, :1]), a[:, :-1]], axis=1)
        return jnp.concatenate([prev, a], axis=2)

    qb = blocks(q)
    kk = with_prev(blocks(k))
    vv = with_prev(blocks(v))
    s = jnp.einsum('bnirhd,bnjrhd->bnrhij', qb, kk, preferred_element_type=jnp.float32) * (Dh ** -0.5)
    i = jnp.arange(wk)[:, None]
    j = jnp.arange(2 * wk)[None, :]
    band = (j >= i) & (j <= i + wk)
    blk = jnp.arange(nb)[:, None, None]
    mask = band[None] & ((blk > 0) | (j >= wk)[None])
    s = jnp.where(mask[:, None, None], s, -jnp.inf)
    m = jnp.max(s, axis=-1, keepdims=True)
    p = jnp.exp(s - m)
    den = jnp.sum(p, axis=-1, keepdims=True)
    o = jnp.einsum('bnrhij,bnjrhd->bnirhd', p, vv, preferred_element_type=jnp.float32)
    o = o / jnp.transpose(den, (0, 1, 4, 2, 3, 5))
    lse = jnp.transpose((m + jnp.log(den))[..., 0], (0, 1, 4, 2, 3))
    return o.reshape(B, s_pad, H, Dh)[:, :S], lse.reshape(B, s_pad, H)[:, :S]


def _dilated_sample(q, k_new, v_new, kv_cache, window, dil):
    Bd, T, H, Dh = q.shape
    L = kv_cache.shape[1]
    wk = window // dil
    keys = jnp.concatenate([kv_cache[:, :, 0], k_new], axis=1)
    vals = jnp.concatenate([kv_cache[:, :, 1], v_new], axis=1)
    idx = L + jnp.arange(T)[:, None] - dil * jnp.arange(wk + 1)[None, :]
    valid = idx >= 0
    idx = jnp.maximum(idx, 0)
    kg = jnp.take(keys, idx, axis=1)
    vg = jnp.take(vals, idx, axis=1)
    s = jnp.einsum('bihd,bikhd->bhik', q, kg, preferred_element_type=jnp.float32) * (Dh ** -0.5)
    s = jnp.where(valid, s, -jnp.inf)
    m = jnp.max(s, axis=-1, keepdims=True)
    p = jnp.exp(s - m)
    den = jnp.sum(p, axis=-1, keepdims=True)
    o = jnp.einsum('bhik,bikhd->bihd', p, vg, preferred_element_type=jnp.float32)
    o = o / jnp.transpose(den, (0, 2, 1, 3))
    lse = jnp.transpose((m + jnp.log(den))[..., 0], (0, 2, 1))
    return o, lse


def _layer(x, norm_w, w_in, lb, hgrn_norm_w, w_proj_a, w_proj_b, w_out, s0, kv_caches):
    B, T, _ = x.shape
    h = _rms_norm(x, norm_w)
    u = jnp.einsum('btd,dn->btn', h, w_in)
    qa, fa, ia, za, qb, kb, vb, zb, ga, gb = _split_in(u)
    y_a, s_fin = _hgrn2_branch(qa, fa, ia, za, lb, s0, hgrn_norm_w)
    qb = qb.reshape(B, T, N_GROUPS, H_G, HEAD_DIM)
    kb = kb.reshape(B, T, N_GROUPS, H_G, HEAD_DIM)
    vb = vb.reshape(B, T, N_GROUPS, H_G, HEAD_DIM)
    outs, lses, kv_new = [], [], []
    for g, (window, dil) in enumerate(DIL_GROUPS):
        q_g, k_g, v_g = qb[:, :, g], kb[:, :, g], vb[:, :, g]
        if kv_caches is None:
            o, lse = _dilated_prompt(q_g, k_g, v_g, window, dil)
            keep = min(window, T)
            kv_new.append(jnp.stack([k_g[:, T - keep:], v_g[:, T - keep:]], axis=2))
        else:
            o, lse = _dilated_sample(q_g, k_g, v_g, kv_caches[g], window, dil)
            kv_new.append(jnp.stack([k_g, v_g], axis=2))
        outs.append(o)
        lses.append(lse)
    alpha = jax.nn.softmax(jnp.stack(lses, axis=0), axis=0)
    y_b = jnp.sum(alpha[..., None] * jnp.stack(outs, axis=0), axis=0)
    y_b = y_b.reshape(B, T, W_BO).astype(x.dtype) * jax.nn.silu(zb)
    merged = (jax.nn.sigmoid(ga) * jnp.einsum('btc,cd->btd', y_a, w_proj_a)
              + jax.nn.sigmoid(gb) * jnp.einsum('btc,cd->btd', y_b, w_proj_b))
    x = x + jnp.einsum('btd,de->bte', merged, w_out)
    return x, kv_new, s_fin


def setup_inputs(seed: int = 0) -> dict:
    key = jax.random.key(seed)
    ks = jax.random.split(key, 14)
    f32 = jnp.float32
    return {
        'x_prompt': jax.random.normal(ks[0], (BATCH, SEQ, D_MODEL), f32),
        'x_sample': jax.random.normal(ks[1], (DEC_BATCH, DEC_SEQ, D_MODEL), f32),
        'cache_kv_w128': jax.random.normal(ks[2], (DEPTH, DEC_BATCH, min(128, PAST_LEN), 2, H_G, HEAD_DIM), f32),
        'cache_kv_w512': jax.random.normal(ks[3], (DEPTH, DEC_BATCH, min(512, PAST_LEN), 2, H_G, HEAD_DIM), f32),
        'cache_kv_w2048': jax.random.normal(ks[4], (DEPTH, DEC_BATCH, min(2048, PAST_LEN), 2, H_G, HEAD_DIM), f32),
        'state_hgrn': 0.5 * jax.random.normal(ks[5], (DEPTH, DEC_BATCH, HGRN_HEADS, HGRN_DK, HGRN_DV), f32),
        'norm_w': 1.0 + 0.02 * jax.random.normal(ks[6], (DEPTH, D_MODEL), f32),
        'w_in': jax.random.normal(ks[7], (DEPTH, D_MODEL, N_IN), f32) * (D_MODEL ** -0.5),
        'lb_logits': 0.1 * jax.random.normal(ks[8], (DEPTH + 1, W_A), f32),
        'hgrn_norm_w': 1.0 + 0.02 * jax.random.normal(ks[9], (DEPTH, HGRN_DV), f32),
        'w_proj_a': jax.random.normal(ks[10], (DEPTH, W_A, D_MODEL), f32) * (W_A ** -0.5),
        'w_proj_b': jax.random.normal(ks[11], (DEPTH, W_BO, D_MODEL), f32) * (W_BO ** -0.5),
        'w_out': jax.random.normal(ks[12], (DEPTH, D_MODEL, D_MODEL), f32) * (D_MODEL ** -0.5),
        'final_norm_w': 1.0 + 0.02 * jax.random.normal(ks[13], (D_MODEL,), f32),
    }


def reference(x_prompt, x_sample, cache_kv_w128, cache_kv_w512, cache_kv_w2048, state_hgrn,
              norm_w, w_in, lb_logits, hgrn_norm_w, w_proj_a, w_proj_b, w_out, final_norm_w):
    lb_all = jnp.cumsum(jax.nn.softmax(lb_logits.astype(jnp.float32), axis=0), axis=0)
    caches = (cache_kv_w128, cache_kv_w512, cache_kv_w2048)
    xp, xs = x_prompt, x_sample
    kvp = ([], [], [])
    kvs = ([], [], [])
    sp, ss = [], []
    for l in range(DEPTH):
        s0_p = jnp.zeros((x_prompt.shape[0], HGRN_HEADS, HGRN_DK, HGRN_DV), jnp.float32)
        xp, kv_p, s_p = _layer(xp, norm_w[l], w_in[l], lb_all[l], hgrn_norm_w[l], w_proj_a[l],
                               w_proj_b[l], w_out[l], s0_p, None)
        xs, kv_s, s_s = _layer(xs, norm_w[l], w_in[l], lb_all[l], hgrn_norm_w[l], w_proj_a[l],
                               w_proj_b[l], w_out[l], state_hgrn[l], (caches[0][l], caches[1][l], caches[2][l]))
        for g in range(N_GROUPS):
            kvp[g].append(kv_p[g])
            kvs[g].append(kv_s[g])
        sp.append(s_p.astype(x_prompt.dtype))
        ss.append(s_s.astype(state_hgrn.dtype))
    y_prompt = _rms_norm(xp, final_norm_w)
    y_sample = _rms_norm(xs, final_norm_w)
    return (y_prompt, y_sample,
            jnp.stack(kvp[0]), jnp.stack(kvp[1]), jnp.stack(kvp[2]), jnp.stack(sp),
            jnp.stack(kvs[0]), jnp.stack(kvs[1]), jnp.stack(kvs[2]), jnp.stack(ss))
```

```python
import functools

import jax
import jax.numpy as jnp
from jax import lax
from jax.experimental import pallas as pl
from jax.experimental.pallas import tpu as pltpu

F32 = jnp.float32
BF16 = jnp.bfloat16

LANES = 128
RMS_EPS = 1e-6
HGRN_CHUNK = 64
DIL_GROUPS = ((128, 1), (512, 4), (2048, 16))
N_GROUPS = len(DIL_GROUPS)
ROW_TILE = 512
VMEM_LIMIT = 56 * 1024 * 1024


def _cparams(semantics, vmem=VMEM_LIMIT):
    return pltpu.CompilerParams(dimension_semantics=semantics, vmem_limit_bytes=vmem)


def _sigmoid(x):
    return 1.0 / (1.0 + jnp.exp(-x))


def _silu(x):
    return x * _sigmoid(x)


def _largest_divisor(n, candidates):
    for c in candidates:
        if n % c == 0:
            return c
    raise ValueError(f"no tile in {candidates} divides {n}")


def _norm_cast_kernel(x_ref, w_ref, o_ref):
    x = x_ref[...]
    ms = jnp.mean(x * x, axis=-1, keepdims=True)
    o_ref[...] = (x * lax.rsqrt(ms + RMS_EPS) * w_ref[...]).astype(o_ref.dtype)


def _norm_cast(x, w):
    m, d = x.shape
    tm = _largest_divisor(m, (ROW_TILE, 256, 128))
    return pl.pallas_call(
        _norm_cast_kernel,
        out_shape=jax.ShapeDtypeStruct((m, d), BF16),
        grid=(m // tm,),
        in_specs=[pl.BlockSpec((tm, d), lambda i: (i, 0)),
                  pl.BlockSpec((1, d), lambda i: (0, 0))],
        out_specs=pl.BlockSpec((tm, d), lambda i: (i, 0)),
        compiler_params=_cparams(("parallel",)),
        name="norm_cast",
    )(x, w.reshape(1, d))


def _cast_kernel(x_ref, o_ref):
    o_ref[...] = x_ref[...].astype(o_ref.dtype)


def _cast_bf16(w):
    r, c = w.shape
    tr = _largest_divisor(r, (512, 256, 128))
    tc = _largest_divisor(c, (2048, 1024, 512, 256, 128))
    return pl.pallas_call(
        _cast_kernel,
        out_shape=jax.ShapeDtypeStruct((r, c), BF16),
        grid=(r // tr, c // tc),
        in_specs=[pl.BlockSpec((tr, tc), lambda i, j: (i, j))],
        out_specs=pl.BlockSpec((tr, tc), lambda i, j: (i, j)),
        compiler_params=_cparams(("parallel", "parallel")),
        name="cast_bf16",
    )(w)


def _inproj_kernel(hp_ref, hs_ref, w_ref, u3_ref, us_ref, uq_ref, *, n_p, nc, h_g, j_lo, j_hi):
    j = pl.program_id(0)
    i = pl.program_id(1)

    @pl.when(i < n_p)
    def _():
        acc = jnp.dot(hp_ref[...], w_ref[...], preferred_element_type=F32)
        for c in range(nc):
            u3_ref[c] = acc[:, c * LANES:(c + 1) * LANES]

    @pl.when(i >= n_p)
    def _():
        acc = jnp.dot(hs_ref[...], w_ref[...], preferred_element_type=F32)
        us_ref[...] = acc

        @pl.when((j >= j_lo) & (j < j_hi))
        def _():
            for c in range(nc):
                uq_ref[:, c // h_g, c % h_g, :] = acc[:, c * LANES:(c + 1) * LANES]


def _inproj(h_p, h_s, w_bf, *, h_g, attn_off, attn_parts):
    mp, d = h_p.shape
    ms = h_s.shape[0]
    n = w_bf.shape[1]
    tm = ROW_TILE
    tn = _largest_divisor(n, (1024, 512, 256, 128))
    n_p, n_s = mp // tm, ms // tm
    nc = tn // LANES
    assert nc % h_g == 0 and (attn_off * LANES) % tn == 0 and (attn_parts * h_g) % nc == 0
    ppt = nc // h_g
    j_lo = attn_off * LANES // tn
    n_aj = attn_parts // ppt
    kern = functools.partial(_inproj_kernel, n_p=n_p, nc=nc, h_g=h_g, j_lo=j_lo, j_hi=j_lo + n_aj)
    return pl.pallas_call(
        kern,
        out_shape=(jax.ShapeDtypeStruct((n // LANES, mp, LANES), F32),
                   jax.ShapeDtypeStruct((ms, n), F32),
                   jax.ShapeDtypeStruct((ms, attn_parts, h_g, LANES), F32)),
        grid=(n // tn, n_p + n_s),
        in_specs=[
            pl.BlockSpec((tm, d), lambda j, i: (jnp.minimum(i, n_p - 1), 0)),
            pl.BlockSpec((tm, d), lambda j, i: (jnp.clip(i - n_p, 0, n_s - 1), 0)),
            pl.BlockSpec((d, tn), lambda j, i: (0, j)),
        ],
        out_specs=(
            pl.BlockSpec((nc, tm, LANES), lambda j, i: (j, jnp.minimum(i, n_p - 1), 0)),
            pl.BlockSpec((tm, tn), lambda j, i: (jnp.clip(i - n_p, 0, n_s - 1), j)),
            pl.BlockSpec((tm, ppt, h_g, LANES),
                         lambda j, i: (jnp.clip(i - n_p, 0, n_s - 1), jnp.clip(j - j_lo, 0, n_aj - 1), 0, 0)),
        ),
        compiler_params=_cparams(("arbitrary", "arbitrary")),
        name="inproj",
    )(h_p, h_s, w_bf)


def _lower_bound(lb_logits, layer):
    mx = jnp.max(lb_logits, axis=0, keepdims=True)
    e = jnp.exp(lb_logits - mx)
    den = jnp.sum(e, axis=0, keepdims=True)
    num = e[0:1]
    for r in range(1, layer + 1):
        num = num + e[r:r + 1]
    return num / den


def _split3_bf16(x):
    hi = x.astype(BF16)
    r1 = x - hi.astype(F32)
    mid = r1.astype(BF16)
    lo = (r1 - mid.astype(F32)).astype(BF16)
    return hi, mid, lo


def _hgrn_prompt_kernel(q_ref, f_ref, i_ref, z_ref, lb_ref, nw_ref, y_ref, sfin_ref, st_ref,
                        *, hb, tb, chunk, layer):
    t = pl.program_id(2)

    @pl.when(t == 0)
    def _():
        st_ref[...] = jnp.zeros_like(st_ref)

    lb_all = _lower_bound(lb_ref[...], layer)
    nw = nw_ref[...]
    row = lax.broadcasted_iota(jnp.int32, (chunk, chunk), 0)
    col = lax.broadcasted_iota(jnp.int32, (chunk, chunk), 1)
    causal = row >= col
    tril = jnp.where(causal, 1.0, 0.0).astype(BF16)
    dk = q_ref.shape[-1]
    qscale = float(dk) ** -0.5

    def chunk_body(c, carry):
        r0 = pl.multiple_of(c * chunk, chunk)
        for hl in range(hb):
            lb = lb_all[:, hl * LANES:(hl + 1) * LANES]
            qa = q_ref[hl, pl.ds(r0, chunk), :]
            fa = f_ref[hl, pl.ds(r0, chunk), :]
            v = i_ref[hl, pl.ds(r0, chunk), :]
            za = z_ref[hl, pl.ds(r0, chunk), :]
            q = _silu(qa) * qscale
            f = lb + (1.0 - lb) * _sigmoid(fa)
            log_f = jnp.log(f)
            k = 1.0 - f
            hi, mid, lo = _split3_bf16(log_f)
            parts = jnp.dot(tril, jnp.concatenate([hi, mid, lo], axis=1),
                            preferred_element_type=F32)
            g = (parts[:, 0:LANES] + parts[:, LANES:2 * LANES]) + parts[:, 2 * LANES:3 * LANES]
            g_last = g[chunk - 1:chunk, :]
            q_dec = (q * jnp.exp(g)).astype(BF16)
            k_rel = (k * jnp.exp(-g)).astype(BF16)
            k_end = (k * jnp.exp(g_last - g)).astype(BF16)
            decay = jnp.exp(g_last)
            vb = v.astype(BF16)
            a = lax.dot_general(q_dec, k_rel, (((1,), (1,)), ((), ())),
                                preferred_element_type=F32)
            a = jnp.where(causal, a, 0.0)
            o_intra = jnp.dot(a.astype(BF16), vb, preferred_element_type=F32)
            st = st_ref[hl]
            o_inter = lax.dot_general(q_dec, st.astype(BF16), (((1,), (1,)), ((), ())),
                                      preferred_element_type=F32)
            ds_t = lax.dot_general(vb, k_end, (((0,), (0,)), ((), ())),
                                   preferred_element_type=F32)
            st_ref[hl] = decay * st + ds_t
            o = o_intra + o_inter
            ms = jnp.mean(o * o, axis=-1, keepdims=True)
            y = (o * lax.rsqrt(ms + RMS_EPS) * nw) * _silu(za)
            y_ref[pl.ds(r0, chunk), hl * LANES:(hl + 1) * LANES] = y.astype(y_ref.dtype)
        return carry

    lax.fori_loop(0, tb // chunk, chunk_body, 0)

    @pl.when(t == pl.num_programs(2) - 1)
    def _():
        for hl in range(hb):
            sfin_ref[0, hl] = st_ref[hl].T


def _hgrn_prompt(u3, lb_logits, nw, *, batch, seq, n_heads, layer):
    hb = _largest_divisor(n_heads, (4, 2, 1))
    tb = _largest_divisor(seq, (1024, 512, 256, 128, 64))
    chunk = HGRN_CHUNK
    assert tb % chunk == 0
    nt = seq // tb
    nhb = n_heads // hb
    kern = functools.partial(_hgrn_prompt_kernel, hb=hb, tb=tb, chunk=chunk, layer=layer)

    def in_spec(part):
        return pl.BlockSpec((hb, tb, LANES), lambda b, h, t: (part * nhb + h, b * nt + t, 0))

    return pl.pallas_call(
        kern,
        out_shape=(jax.ShapeDtypeStruct((batch * seq, n_heads * LANES), BF16),
                   jax.ShapeDtypeStruct((batch, n_heads, LANES, LANES), F32)),
        grid=(batch, nhb, nt),
        in_specs=[in_spec(0), in_spec(1), in_spec(2), in_spec(3),
                  pl.BlockSpec((lb_logits.shape[0], hb * LANES), lambda b, h, t: (0, h)),
                  pl.BlockSpec((1, LANES), lambda b, h, t: (0, 0))],
        out_specs=(pl.BlockSpec((tb, hb * LANES), lambda b, h, t: (b * nt + t, h)),
                   pl.BlockSpec((1, hb, LANES, LANES), lambda b, h, t: (b, h, 0, 0))),
        scratch_shapes=[pltpu.VMEM((hb, LANES, LANES), F32)],
        compiler_params=_cparams(("parallel", "parallel", "arbitrary")),
        name="hgrn_prompt",
    )(u3, u3, u3, u3, lb_logits, nw)


def _hgrn_sample_kernel(q_ref, f_ref, i_ref, z_ref, lb_ref, nw_ref, s_ref, y_ref, so_ref, o_scr,
                        *, bb, td, layer):
    lb = _lower_bound(lb_ref[...], layer)
    dk = q_ref.shape[-1]
    q = _silu(q_ref[...]) * (float(dk) ** -0.5)
    f = lb + (1.0 - lb) * _sigmoid(f_ref[...])
    k = 1.0 - f
    v = i_ref[...]
    q_t, f_t, k_t = q.T, f.T, k.T
    for bl in range(bb):
        s = s_ref[bl, 0]
        for t in range(td):
            j = bl * td + t
            s = f_t[:, j:j + 1] * s + k_t[:, j:j + 1] * v[j:j + 1, :]
            o_scr[j:j + 1, :] = jnp.sum(s * q_t[:, j:j + 1], axis=0, keepdims=True)
        so_ref[bl, 0] = s
    o = o_scr[...]
    ms = jnp.mean(o * o, axis=-1, keepdims=True)
    y = (o * lax.rsqrt(ms + RMS_EPS) * nw_ref[...]) * _silu(z_ref[...])
    y_ref[...] = y.astype(y_ref.dtype)


def _hgrn_sample(us, state, lb_logits, nw, *, dec_batch, td, n_heads, layer):
    rows = LANES
    assert rows % td == 0
    bb = rows // td
    assert dec_batch % bb == 0
    kern = functools.partial(_hgrn_sample_kernel, bb=bb, td=td, layer=layer)

    def in_spec(part):
        return pl.BlockSpec((rows, LANES), lambda bi, h: (bi, part * n_heads + h))

    return pl.pallas_call(
        kern,
        out_shape=(jax.ShapeDtypeStruct((dec_batch * td, n_heads * LANES), BF16),
                   jax.ShapeDtypeStruct(state.shape, F32)),
        grid=(dec_batch // bb, n_heads),
        in_specs=[in_spec(0), in_spec(1), in_spec(2), in_spec(3),
                  pl.BlockSpec((lb_logits.shape[0], LANES), lambda bi, h: (0, h)),
                  pl.BlockSpec((1, LANES), lambda bi, h: (0, 0)),
                  pl.BlockSpec((bb, 1, LANES, LANES), lambda bi, h: (bi, h, 0, 0))],
        out_specs=(pl.BlockSpec((rows, LANES), lambda bi, h: (bi, h)),
                   pl.BlockSpec((bb, 1, LANES, LANES), lambda bi, h: (bi, h, 0, 0))),
        scratch_shapes=[pltpu.VMEM((rows, LANES), F32)],
        compiler_params=_cparams(("parallel", "parallel")),
        name="hgrn_sample",
    )(us, us, us, us, lb_logits, nw, state)


def _attn_prompt_kernel(q_ref, k_ref, v_ref, o_ref, l_ref, *, dil, wk, nb, scale):
    ii = lax.broadcasted_iota(jnp.int32, (wk, wk), 0)
    jj = lax.broadcasted_iota(jnp.int32, (wk, wk), 1)
    mask_prev = jj >= ii
    mask_own = jj <= ii
    neg_inf = -jnp.inf
    contract_last = (((1,), (1,)), ((), ()))
    span = wk * dil

    def rows(start):
        return pl.ds(start, wk, stride=dil) if dil > 1 else pl.ds(start, wk)

    def block(idx, carry):
        n, r = idx // dil, idx % dil
        own = rows(n * span + r)
        prev = rows(jnp.maximum(n - 1, 0) * span + r)
        q = q_ref[0, own, :].astype(BF16)
        k_own = k_ref[0, own, :].astype(BF16)
        k_prev = k_ref[0, prev, :].astype(BF16)
        v_own = v_ref[0, own, :].astype(BF16)
        v_prev = v_ref[0, prev, :].astype(BF16)
        s_own = lax.dot_general(q, k_own, contract_last, preferred_element_type=F32) * scale
        s_prev = lax.dot_general(q, k_prev, contract_last, preferred_element_type=F32) * scale
        no_prev = jnp.where(n > 0, 0.0, neg_inf)
        s_own = jnp.where(mask_own, s_own, neg_inf)
        s_prev = jnp.where(mask_prev, s_prev + no_prev, neg_inf)
        m = jnp.maximum(jnp.max(s_own, axis=-1, keepdims=True),
                        jnp.max(s_prev, axis=-1, keepdims=True))
        p_own = jnp.exp(s_own - m)
        p_prev = jnp.exp(s_prev - m)
        den = jnp.sum(p_own, axis=-1, keepdims=True) + jnp.sum(p_prev, axis=-1, keepdims=True)
        o = (jnp.dot(p_own.astype(BF16), v_own, preferred_element_type=F32)
             + jnp.dot(p_prev.astype(BF16), v_prev, preferred_element_type=F32))
        o_ref[0, own, :] = o / den
        l_ref[0, own, :] = jnp.broadcast_to(m + jnp.log(den), (wk, LANES))
        return carry

    lax.fori_loop(0, nb * dil, block, 0)


def _attn_prompt_group(u3, *, g, batch, seq, h_g, q_off, k_off, v_off):
    window, dil = DIL_GROUPS[g]
    wk = window // dil
    assert seq % window == 0
    nb = seq // window
    kern = functools.partial(_attn_prompt_kernel, dil=dil, wk=wk, nb=nb, scale=float(LANES) ** -0.5)

    def in_spec(off):
        return pl.BlockSpec((1, seq, LANES), lambda b, h: (off + g * h_g + h, b, 0))

    out_sds = jax.ShapeDtypeStruct((h_g, batch * seq, LANES), F32)
    return pl.pallas_call(
        kern,
        out_shape=(out_sds, out_sds),
        grid=(batch, h_g),
        in_specs=[in_spec(q_off), in_spec(k_off), in_spec(v_off)],
        out_specs=(pl.BlockSpec((1, seq, LANES), lambda b, h: (h, b, 0)),
                   pl.BlockSpec((1, seq, LANES), lambda b, h: (h, b, 0))),
        compiler_params=_cparams(("parallel", "parallel")),
        name=f"attn_prompt_w{window}",
    )(u3, u3, u3)


def _attn_combine_kernel(o0_ref, o1_ref, o2_ref, l0_ref, l1_ref, l2_ref, z_ref, y_ref):
    l0, l1, l2 = l0_ref[0], l1_ref[0], l2_ref[0]
    m = jnp.maximum(jnp.maximum(l0, l1), l2)
    e0, e1, e2 = jnp.exp(l0 - m), jnp.exp(l1 - m), jnp.exp(l2 - m)
    inv = 1.0 / (e0 + e1 + e2)
    y = (e0 * inv) * o0_ref[0] + (e1 * inv) * o1_ref[0] + (e2 * inv) * o2_ref[0]
    y_ref[...] = (y * _silu(z_ref[0])).astype(y_ref.dtype)


def _attn_combine(outs, lses, u3, *, z_off):
    h_g, mp, _ = outs[0].shape
    tr = _largest_divisor(mp, (1024, 512, 256, 128))
    spec = pl.BlockSpec((1, tr, LANES), lambda h, i: (h, i, 0))
    return pl.pallas_call(
        _attn_combine_kernel,
        out_shape=jax.ShapeDtypeStruct((mp, h_g * LANES), BF16),
        grid=(h_g, mp // tr),
        in_specs=[spec] * 6 + [pl.BlockSpec((1, tr, LANES), lambda h, i: (z_off + h, i, 0))],
        out_specs=pl.BlockSpec((tr, LANES), lambda h, i: (i, h)),
        compiler_params=_cparams(("parallel", "parallel")),
        name="attn_combine",
    )(*outs, *lses, u3)


def _attn_sample_kernel(uq_ref, c0_ref, c1_ref, c2_ref, y_ref, *, td, h_g):
    c_refs = (c0_ref, c1_ref, c2_ref)
    b = pl.program_id(0)
    rows_blk = uq_ref.shape[0]
    per_blk = rows_blk // td
    ro = (b % per_blk) * td
    scale = float(LANES) ** -0.5
    neg_inf = -jnp.inf
    contract_last = (((1,), (1,)), ((), ()))
    expand = jnp.ones((h_g, LANES), BF16)
    q_part, k_part, v_part, z_part = 0, N_GROUPS, 2 * N_GROUPS, 3 * N_GROUPS

    for i in range(td):
        r = ro + i
        stats = []
        for g, (window, dil) in enumerate(DIL_GROUPS):
            wk = window // dil
            c_ref = c_refs[g]
            q_t = uq_ref[r, q_part + g]
            if dil == 1:
                k_c, v_c = c_ref[0, :, 0], c_ref[0, :, 1]
                new_js = tuple(range(i + 1))
            else:
                k_c, v_c = c_ref[0, :, i, 0], c_ref[0, :, i, 1]
                new_js = (i,)
            s = lax.dot_general(k_c.reshape(wk * h_g, LANES).astype(BF16), q_t.astype(BF16),
                                contract_last, preferred_element_type=F32) * scale
            s = s.reshape(wk, h_g, h_g)
            kh = lax.broadcasted_iota(jnp.int32, (wk, h_g, h_g), 1)
            qh = lax.broadcasted_iota(jnp.int32, (wk, h_g, h_g), 2)
            valid = kh == qh
            if dil == 1:
                valid = valid & (lax.broadcasted_iota(jnp.int32, (wk, h_g, h_g), 0) >= i)
            s = jnp.where(valid, s, neg_inf)
            m = jnp.max(jnp.max(s, axis=0), axis=-1, keepdims=True)
            s_new = [jnp.sum(uq_ref[ro + jn, k_part + g] * q_t, axis=-1, keepdims=True) * scale
                     for jn in new_js]
            for sn in s_new:
                m = jnp.maximum(m, sn)
            p = jnp.exp(s - m[None])
            den = jnp.sum(jnp.sum(p, axis=0), axis=-1, keepdims=True)
            p_new = [jnp.exp(sn - m) for sn in s_new]
            for pn in p_new:
                den = den + pn
            stats.append((p, p_new, new_js, den, m + jnp.log(den), v_c))
        lmax = jnp.maximum(jnp.maximum(stats[0][4], stats[1][4]), stats[2][4])
        es = [jnp.exp(st[4] - lmax) for st in stats]
        inv = 1.0 / (es[0] + es[1] + es[2])
        y = jnp.zeros((h_g, LANES), F32)
        for g in range(N_GROUPS):
            p, p_new, new_js, den, _, v_c = stats[g]
            wk = p.shape[0]
            wgt = (es[g] * inv) / den
            pw = (p * wgt[None]).astype(BF16).reshape(wk * h_g, h_g)
            pb = jnp.dot(pw, expand, preferred_element_type=F32)
            y = y + jnp.sum(pb.reshape(wk, h_g, LANES) * v_c, axis=0)
            for jn, pn in zip(new_js, p_new):
                y = y + (pn * wgt) * uq_ref[ro + jn, v_part + g]
        y_ref[r] = y * _silu(uq_ref[r, z_part])


def _attn_sample(uq, caches, *, dec_batch, td, h_g):
    rows_blk = 8
    assert rows_blk % td == 0 and (dec_batch * td) % rows_blk == 0
    per_blk = rows_blk // td
    parts = uq.shape[1]
    views, c_specs = [], []
    for (window, dil), c in zip(DIL_GROUPS, caches):
        assert c.shape[1] == window, "window buffers must hold a full window"
        wk = window // dil
        if dil == 1:
            views.append(c)
            c_specs.append(pl.BlockSpec((1, wk, 2, h_g, LANES), lambda b: (b, 0, 0, 0, 0)))
        else:
            assert td <= dil
            views.append(c.reshape(dec_batch, wk, dil, 2, h_g, LANES))
            c_specs.append(pl.BlockSpec((1, wk, td, 2, h_g, LANES), lambda b: (b, 0, 0, 0, 0, 0)))
    kern = functools.partial(_attn_sample_kernel, td=td, h_g=h_g)
    return pl.pallas_call(
        kern,
        out_shape=jax.ShapeDtypeStruct((dec_batch * td, h_g, LANES), F32),
        grid=(dec_batch,),
        in_specs=[pl.BlockSpec((rows_blk, parts, h_g, LANES), lambda b: (b // per_blk, 0, 0, 0))] + c_specs,
        out_specs=pl.BlockSpec((rows_blk, h_g, LANES), lambda b: (b // per_blk, 0, 0)),
        compiler_params=_cparams(("arbitrary",)),
        name="attn_sample",
    )(uq, *views)


def _merge_kernel(ya_ref, yb_ref, wa_ref, wb_ref, ga_ref, gb_ref, o_ref, *, nc, blocked_gates):
    pa = jnp.dot(ya_ref[...].astype(BF16), wa_ref[...], preferred_element_type=F32)
    pb = jnp.dot(yb_ref[...].astype(BF16), wb_ref[...], preferred_element_type=F32)
    for c in range(nc):
        ls = slice(c * LANES, (c + 1) * LANES)
        ga = ga_ref[c] if blocked_gates else ga_ref[:, ls]
        gb = gb_ref[c] if blocked_gates else gb_ref[:, ls]
        o_ref[:, ls] = (_sigmoid(ga) * pa[:, ls] + _sigmoid(gb) * pb[:, ls]).astype(o_ref.dtype)


def _merge(ya, yb, wa_bf, wb_bf, gates, *, ga_off, gb_off, blocked_gates):
    m = ya.shape[0]
    d = wa_bf.shape[1]
    tm = ROW_TILE
    tn = _largest_divisor(d, (1024, 512, 256, 128))
    nc = tn // LANES
    assert (ga_off * LANES) % tn == 0 and (gb_off * LANES) % tn == 0
    ga_blk, gb_blk = ga_off * LANES // tn, gb_off * LANES // tn
    if blocked_gates:
        ga_spec = pl.BlockSpec((nc, tm, LANES), lambda j, i: (ga_blk + j, i, 0))
        gb_spec = pl.BlockSpec((nc, tm, LANES), lambda j, i: (gb_blk + j, i, 0))
    else:
        ga_spec = pl.BlockSpec((tm, tn), lambda j, i: (i, ga_blk + j))
        gb_spec = pl.BlockSpec((tm, tn), lambda j, i: (i, gb_blk + j))
    kern = functools.partial(_merge_kernel, nc=nc, blocked_gates=blocked_gates)
    return pl.pallas_call(
        kern,
        out_shape=jax.ShapeDtypeStruct((m, d), BF16),
        grid=(d // tn, m // tm),
        in_specs=[pl.BlockSpec((tm, ya.shape[1]), lambda j, i: (i, 0)),
                  pl.BlockSpec((tm, yb.shape[1]), lambda j, i: (i, 0)),
                  pl.BlockSpec((wa_bf.shape[0], tn), lambda j, i: (0, j)),
                  pl.BlockSpec((wb_bf.shape[0], tn), lambda j, i: (0, j)),
                  ga_spec, gb_spec],
        out_specs=pl.BlockSpec((tm, tn), lambda j, i: (i, j)),
        compiler_params=_cparams(("parallel", "parallel")),
        name="merge_blocked" if blocked_gates else "merge_rows",
    )(ya, yb, wa_bf, wb_bf, gates, gates)


def _outproj_kernel(mg_ref, w_ref, x_ref, fw_ref, y_ref, *, nj, tn):
    j = pl.program_id(1)
    val = x_ref[...] + jnp.dot(mg_ref[...], w_ref[...], preferred_element_type=F32)
    for jj in range(nj):
        @pl.when(j == jj)
        def _(jj=jj):
            y_ref[:, jj * tn:(jj + 1) * tn] = val

    @pl.when(j == nj - 1)
    def _():
        full = y_ref[...]
        ms = jnp.mean(full * full, axis=-1, keepdims=True)
        y_ref[...] = full * lax.rsqrt(ms + RMS_EPS) * fw_ref[...]


def _outproj(merged, w_bf, x, fw):
    m, d = x.shape
    tm = ROW_TILE
    tn = _largest_divisor(d, (512, 256, 128))
    nj = d // tn
    kern = functools.partial(_outproj_kernel, nj=nj, tn=tn)
    return pl.pallas_call(
        kern,
        out_shape=jax.ShapeDtypeStruct((m, d), F32),
        grid=(m // tm, nj),
        in_specs=[pl.BlockSpec((tm, d), lambda i, j: (i, 0)),
                  pl.BlockSpec((d, tn), lambda i, j: (0, j)),
                  pl.BlockSpec((tm, tn), lambda i, j: (i, j)),
                  pl.BlockSpec((1, d), lambda i, j: (0, 0))],
        out_specs=pl.BlockSpec((tm, d), lambda i, j: (i, 0)),
        compiler_params=_cparams(("parallel", "arbitrary")),
        name="outproj_norm",
    )(merged, w_bf, x, fw.reshape(1, d))


def _kv_prompt(u3, *, g, batch, seq, h_g, k_off, v_off):
    window, _ = DIL_GROUPS[g]
    keep = min(window, seq)

    def heads(off):
        slab = u3[off + g * h_g: off + (g + 1) * h_g].reshape(h_g, batch, seq, LANES)
        return slab[:, :, seq - keep:, :]

    kv = jnp.stack([heads(k_off), heads(v_off)], axis=0)
    return jnp.transpose(kv, (2, 3, 0, 1, 4))[None]


def kernel(x_prompt, x_sample, cache_kv_w128, cache_kv_w512, cache_kv_w2048, state_hgrn,
           norm_w, w_in, lb_logits, hgrn_norm_w, w_proj_a, w_proj_b, w_out, final_norm_w):
    depth = norm_w.shape[0]
    assert depth == 1, "single-layer step"
    layer = 0
    batch, seq, d = x_prompt.shape
    dec_batch, td, _ = x_sample.shape
    w_a = w_proj_a.shape[1]
    w_bo = w_proj_b.shape[1]
    n_heads = w_a // LANES
    h_g = w_bo // LANES
    n_b = N_GROUPS * h_g
    n_d = d // LANES
    qb_off = 4 * n_heads
    kb_off, vb_off = qb_off + n_b, qb_off + 2 * n_b
    zb_off = qb_off + 3 * n_b
    ga_off = zb_off + h_g
    gb_off = ga_off + n_d
    assert (gb_off + n_d) * LANES == w_in.shape[2]

    xp = x_prompt.reshape(batch * seq, d)
    xs = x_sample.reshape(dec_batch * td, d)
    caches = (cache_kv_w128[layer], cache_kv_w512[layer], cache_kv_w2048[layer])

    w_in_bf = _cast_bf16(w_in[layer])
    wa_bf = _cast_bf16(w_proj_a[layer])
    wb_bf = _cast_bf16(w_proj_b[layer])
    wo_bf = _cast_bf16(w_out[layer])

    h_p = _norm_cast(xp, norm_w[layer])
    h_s = _norm_cast(xs, norm_w[layer])
    u3, us, uq = _inproj(h_p, h_s, w_in_bf, h_g=h_g, attn_off=qb_off, attn_parts=3 * N_GROUPS + 1)

    nw = hgrn_norm_w[layer].reshape(1, LANES)
    ya_p, s_p = _hgrn_prompt(u3, lb_logits, nw, batch=batch, seq=seq, n_heads=n_heads, layer=layer)
    ya_s, s_s = _hgrn_sample(us, state_hgrn[layer], lb_logits, nw,
                             dec_batch=dec_batch, td=td, n_heads=n_heads, layer=layer)

    outs, lses = [], []
    for g in range(N_GROUPS):
        o, lse = _attn_prompt_group(u3, g=g, batch=batch, seq=seq, h_g=h_g,
                                    q_off=qb_off, k_off=kb_off, v_off=vb_off)
        outs.append(o)
        lses.append(lse)
    yb_p = _attn_combine(outs, lses, u3, z_off=zb_off)
    yb_s = _attn_sample(uq, caches, dec_batch=dec_batch, td=td, h_g=h_g)
    yb_s = yb_s.reshape(dec_batch * td, h_g * LANES)

    mg_p = _merge(ya_p, yb_p, wa_bf, wb_bf, u3, ga_off=ga_off, gb_off=gb_off, blocked_gates=True)
    mg_s = _merge(ya_s, yb_s, wa_bf, wb_bf, us, ga_off=ga_off, gb_off=gb_off, blocked_gates=False)
    y_p = _outproj(mg_p, wo_bf, xp, final_norm_w)
    y_s = _outproj(mg_s, wo_bf, xs, final_norm_w)

    kv_p = [_kv_prompt(u3, g=g, batch=batch, seq=seq, h_g=h_g, k_off=kb_off, v_off=vb_off)
            for g in range(N_GROUPS)]
    kv_s = [jnp.stack([uq[:, N_GROUPS + g], uq[:, 2 * N_GROUPS + g]], axis=1)
            .reshape(1, dec_batch, td, 2, h_g, LANES) for g in range(N_GROUPS)]

    return (y_p.reshape(batch, seq, d), y_s.reshape(dec_batch, td, d),
            kv_p[0], kv_p[1], kv_p[2], s_p[None],
            kv_s[0], kv_s[1], kv_s[2], s_s[None])
```

```python
import functools

import jax
import jax.numpy as jnp
from jax import lax
from jax.experimental import pallas as pl
from jax.experimental.pallas import tpu as pltpu

F32 = jnp.float32
BF16 = jnp.bfloat16

LANES = 128
RMS_EPS = 1e-6
HGRN_CHUNK = 64
DIL_GROUPS = ((128, 1), (512, 4), (2048, 16))
N_GROUPS = len(DIL_GROUPS)
ROW_TILE = 512
VMEM_LIMIT = 56 * 1024 * 1024


def _cparams(semantics, vmem=VMEM_LIMIT):
    return pltpu.CompilerParams(dimension_semantics=semantics, vmem_limit_bytes=vmem)


def _sigmoid(x):
    return 1.0 / (1.0 + jnp.exp(-x))


def _silu(x):
    return x * _sigmoid(x)


def _largest_divisor(n, candidates):
    for c in candidates:
        if n % c == 0:
            return c
    raise ValueError(f"no tile in {candidates} divides {n}")


def _norm_cast_kernel(x_ref, w_ref, o_ref):
    x = x_ref[...]
    ms = jnp.mean(x * x, axis=-1, keepdims=True)
    o_ref[...] = (x * lax.rsqrt(ms + RMS_EPS) * w_ref[...]).astype(o_ref.dtype)


def _norm_cast(x, w):
    m, d = x.shape
    tm = _largest_divisor(m, (ROW_TILE, 256, 128))
    return pl.pallas_call(
        _norm_cast_kernel,
        out_shape=jax.ShapeDtypeStruct((m, d), BF16),
        grid=(m // tm,),
        in_specs=[pl.BlockSpec((tm, d), lambda i: (i, 0)),
                  pl.BlockSpec((1, d), lambda i: (0, 0))],
        out_specs=pl.BlockSpec((tm, d), lambda i: (i, 0)),
        compiler_params=_cparams(("parallel",)),
        name="norm_cast",
    )(x, w.reshape(1, d))


def _cast_kernel(x_ref, o_ref):
    o_ref[...] = x_ref[...].astype(o_ref.dtype)


def _cast_bf16(w):
    r, c = w.shape
    tr = _largest_divisor(r, (512, 256, 128))
    tc = _largest_divisor(c, (2048, 1024, 512, 256, 128))
    return pl.pallas_call(
        _cast_kernel,
        out_shape=jax.ShapeDtypeStruct((r, c), BF16),
        grid=(r // tr, c // tc),
        in_specs=[pl.BlockSpec((tr, tc), lambda i, j: (i, j))],
        out_specs=pl.BlockSpec((tr, tc), lambda i, j: (i, j)),
        compiler_params=_cparams(("parallel", "parallel")),
        name="cast_bf16",
    )(w)


def _inproj_kernel(hp_ref, hs_ref, w_ref, u3_ref, us_ref, uq_ref, *, n_p, nc, h_g, j_lo, j_hi):
    j = pl.program_id(0)
    i = pl.program_id(1)

    @pl.when(i < n_p)
    def _():
        acc = jnp.dot(hp_ref[...], w_ref[...], preferred_element_type=F32)
        for c in range(nc):
            u3_ref[c] = acc[:, c * LANES:(c + 1) * LANES]

    @pl.when(i >= n_p)
    def _():
        acc = jnp.dot(hs_ref[...], w_ref[...], preferred_element_type=F32)
        us_ref[...] = acc

        @pl.when((j >= j_lo) & (j < j_hi))
        def _():
            for c in range(nc):
                uq_ref[:, c // h_g, c % h_g, :] = acc[:, c * LANES:(c + 1) * LANES]


def _inproj(h_p, h_s, w_bf, *, h_g, attn_off, attn_parts):
    mp, d = h_p.shape
    ms = h_s.shape[0]
    n = w_bf.shape[1]
    tm = ROW_TILE
    tn = _largest_divisor(n, (1024, 512, 256, 128))
    n_p, n_s = mp // tm, ms // tm
    nc = tn // LANES
    assert nc % h_g == 0 and (attn_off * LANES) % tn == 0 and (attn_parts * h_g) % nc == 0
    ppt = nc // h_g
    j_lo = attn_off * LANES // tn
    n_aj = attn_parts // ppt
    kern = functools.partial(_inproj_kernel, n_p=n_p, nc=nc, h_g=h_g, j_lo=j_lo, j_hi=j_lo + n_aj)
    return pl.pallas_call(
        kern,
        out_shape=(jax.ShapeDtypeStruct((n // LANES, mp, LANES), F32),
                   jax.ShapeDtypeStruct((ms, n), F32),
                   jax.ShapeDtypeStruct((ms, attn_parts, h_g, LANES), F32)),
        grid=(n // tn, n_p + n_s),
        in_specs=[
            pl.BlockSpec((tm, d), lambda j, i: (jnp.minimum(i, n_p - 1), 0)),
            pl.BlockSpec((tm, d), lambda j, i: (jnp.clip(i - n_p, 0, n_s - 1), 0)),
            pl.BlockSpec((d, tn), lambda j, i: (0, j)),
        ],
        out_specs=(
            pl.BlockSpec((nc, tm, LANES), lambda j, i: (j, jnp.minimum(i, n_p - 1), 0)),
            pl.BlockSpec((tm, tn), lambda j, i: (jnp.clip(i - n_p, 0, n_s - 1), j)),
            pl.BlockSpec((tm, ppt, h_g, LANES),
                         lambda j, i: (jnp.clip(i - n_p, 0, n_s - 1), jnp.clip(j - j_lo, 0, n_aj - 1), 0, 0)),
        ),
        compiler_params=_cparams(("arbitrary", "arbitrary")),
        name="inproj",
    )(h_p, h_s, w_bf)


def _lower_bound(lb_logits, layer):
    mx = jnp.max(lb_logits, axis=0, keepdims=True)
    e = jnp.exp(lb_logits - mx)
    den = jnp.sum(e, axis=0, keepdims=True)
    num = e[0:1]
    for r in range(1, layer + 1):
        num = num + e[r:r + 1]
    return num / den


def _split3_bf16(x):
    hi = x.astype(BF16)
    r1 = x - hi.astype(F32)
    mid = r1.astype(BF16)
    lo = (r1 - mid.astype(F32)).astype(BF16)
    return hi, mid, lo


def _hgrn_prompt_kernel(q_ref, f_ref, i_ref, z_ref, lb_ref, nw_ref, y_ref, sfin_ref, st_ref,
                        *, hb, tb, chunk, cpb, layer):
    t = pl.program_id(2)

    @pl.when(t == 0)
    def _():
        st_ref[...] = jnp.zeros_like(st_ref)

    lb_all = _lower_bound(lb_ref[...], layer)
    nw = nw_ref[...]
    rb = cpb * chunk
    row = lax.broadcasted_iota(jnp.int32, (rb, rb), 0)
    col = lax.broadcasted_iota(jnp.int32, (rb, rb), 1)
    causal = (row >= col) & (row // chunk == col // chunk)
    tril = jnp.where(causal, 1.0, 0.0).astype(BF16)
    dk = q_ref.shape[-1]
    qscale = float(dk) ** -0.5
    contract_last = (((1,), (1,)), ((), ()))
    contract_first = (((0,), (0,)), ((), ()))

    def block_body(blk, carry):
        r0 = pl.multiple_of(blk * rb, rb)
        for hl in range(hb):
            lb = lb_all[:, hl * LANES:(hl + 1) * LANES]
            qa = q_ref[hl, pl.ds(r0, rb), :]
            fa = f_ref[hl, pl.ds(r0, rb), :]
            v = i_ref[hl, pl.ds(r0, rb), :]
            za = z_ref[hl, pl.ds(r0, rb), :]
            q = _silu(qa) * qscale
            f = lb + (1.0 - lb) * _sigmoid(fa)
            log_f = jnp.log(f)
            k = 1.0 - f
            hi, mid, lo = _split3_bf16(log_f)
            parts = jnp.dot(tril, jnp.concatenate([hi, mid, lo], axis=1),
                            preferred_element_type=F32)
            g = (parts[:, 0:LANES] + parts[:, LANES:2 * LANES]) + parts[:, 2 * LANES:3 * LANES]
            g_last = [g[(c + 1) * chunk - 1:(c + 1) * chunk, :] for c in range(cpb)]
            g_last_rows = jnp.concatenate(
                [jnp.broadcast_to(gl, (chunk, LANES)) for gl in g_last], axis=0)
            q_dec = (q * jnp.exp(g)).astype(BF16)
            k_rel = (k * jnp.exp(-g)).astype(BF16)
            k_end = (k * jnp.exp(g_last_rows - g)).astype(BF16)
            vb = v.astype(BF16)
            a = lax.dot_general(q_dec, k_rel, contract_last, preferred_element_type=F32)
            a = jnp.where(causal, a, 0.0)
            o_intra = jnp.dot(a.astype(BF16), vb, preferred_element_type=F32)
            st = st_ref[hl]
            o_inter = []
            for c in range(cpb):
                rows = slice(c * chunk, (c + 1) * chunk)
                o_inter.append(lax.dot_general(q_dec[rows], st.astype(BF16), contract_last,
                                               preferred_element_type=F32))
                ds_t = lax.dot_general(vb[rows], k_end[rows], contract_first,
                                       preferred_element_type=F32)
                st = jnp.exp(g_last[c]) * st + ds_t
            st_ref[hl] = st
            o = o_intra + jnp.concatenate(o_inter, axis=0)
            ms = jnp.mean(o * o, axis=-1, keepdims=True)
            y = (o * lax.rsqrt(ms + RMS_EPS) * nw) * _silu(za)
            y_ref[pl.ds(r0, rb), hl * LANES:(hl + 1) * LANES] = y.astype(y_ref.dtype)
        return carry

    lax.fori_loop(0, tb // rb, block_body, 0)

    @pl.when(t == pl.num_programs(2) - 1)
    def _():
        for hl in range(hb):
            sfin_ref[0, hl] = st_ref[hl].T


def _hgrn_prompt(u3, lb_logits, nw, *, batch, seq, n_heads, layer):
    hb = _largest_divisor(n_heads, (4, 2, 1))
    tb = _largest_divisor(seq, (1024, 512, 256, 128, 64))
    chunk = HGRN_CHUNK
    assert tb % chunk == 0
    nt = seq // tb
    nhb = n_heads // hb
    cpb = _largest_divisor(tb // chunk, (4, 2, 1))
    kern = functools.partial(_hgrn_prompt_kernel, hb=hb, tb=tb, chunk=chunk, cpb=cpb, layer=layer)

    def in_spec(part):
        return pl.BlockSpec((hb, tb, LANES), lambda b, h, t: (part * nhb + h, b * nt + t, 0))

    return pl.pallas_call(
        kern,
        out_shape=(jax.ShapeDtypeStruct((batch * seq, n_heads * LANES), BF16),
                   jax.ShapeDtypeStruct((batch, n_heads, LANES, LANES), F32)),
        grid=(batch, nhb, nt),
        in_specs=[in_spec(0), in_spec(1), in_spec(2), in_spec(3),
                  pl.BlockSpec((lb_logits.shape[0], hb * LANES), lambda b, h, t: (0, h)),
                  pl.BlockSpec((1, LANES), lambda b, h, t: (0, 0))],
        out_specs=(pl.BlockSpec((tb, hb * LANES), lambda b, h, t: (b * nt + t, h)),
                   pl.BlockSpec((1, hb, LANES, LANES), lambda b, h, t: (b, h, 0, 0))),
        scratch_shapes=[pltpu.VMEM((hb, LANES, LANES), F32)],
        compiler_params=_cparams(("parallel", "parallel", "arbitrary")),
        name="hgrn_prompt",
    )(u3, u3, u3, u3, lb_logits, nw)


def _hgrn_sample_kernel(q_ref, f_ref, i_ref, z_ref, lb_ref, nw_ref, s_ref, y_ref, so_ref, o_scr,
                        *, bb, td, layer):
    lb = _lower_bound(lb_ref[...], layer)
    dk = q_ref.shape[-1]
    q = _silu(q_ref[...]) * (float(dk) ** -0.5)
    f = lb + (1.0 - lb) * _sigmoid(f_ref[...])
    k = 1.0 - f
    v = i_ref[...]
    q_t, f_t, k_t = q.T, f.T, k.T
    for bl in range(bb):
        s = s_ref[bl, 0]
        for t in range(td):
            j = bl * td + t
            s = f_t[:, j:j + 1] * s + k_t[:, j:j + 1] * v[j:j + 1, :]
            o_scr[j:j + 1, :] = jnp.sum(s * q_t[:, j:j + 1], axis=0, keepdims=True)
        so_ref[bl, 0] = s
    o = o_scr[...]
    ms = jnp.mean(o * o, axis=-1, keepdims=True)
    y = (o * lax.rsqrt(ms + RMS_EPS) * nw_ref[...]) * _silu(z_ref[...])
    y_ref[...] = y.astype(y_ref.dtype)


def _hgrn_sample(us, state, lb_logits, nw, *, dec_batch, td, n_heads, layer):
    rows = LANES
    assert rows % td == 0
    bb = rows // td
    assert dec_batch % bb == 0
    kern = functools.partial(_hgrn_sample_kernel, bb=bb, td=td, layer=layer)

    def in_spec(part):
        return pl.BlockSpec((rows, LANES), lambda bi, h: (bi, part * n_heads + h))

    return pl.pallas_call(
        kern,
        out_shape=(jax.ShapeDtypeStruct((dec_batch * td, n_heads * LANES), BF16),
                   jax.ShapeDtypeStruct(state.shape, F32)),
        grid=(dec_batch // bb, n_heads),
        in_specs=[in_spec(0), in_spec(1), in_spec(2), in_spec(3),
                  pl.BlockSpec((lb_logits.shape[0], LANES), lambda bi, h: (0, h)),
                  pl.BlockSpec((1, LANES), lambda bi, h: (0, 0)),
                  pl.BlockSpec((bb, 1, LANES, LANES), lambda bi, h: (bi, h, 0, 0))],
        out_specs=(pl.BlockSpec((rows, LANES), lambda bi, h: (bi, h)),
                   pl.BlockSpec((bb, 1, LANES, LANES), lambda bi, h: (bi, h, 0, 0))),
        scratch_shapes=[pltpu.VMEM((rows, LANES), F32)],
        compiler_params=_cparams(("parallel", "parallel")),
        name="hgrn_sample",
    )(us, us, us, us, lb_logits, nw, state)


def _attn_prompt_kernel(q_ref, k_ref, v_ref, z_ref, y_ref, m_acc, den_acc, o_acc, *, seq, scale):
    g = pl.program_id(2)
    neg_inf = -jnp.inf
    contract_last = (((1,), (1,)), ((), ()))

    def run_group(first, window, dil):
        wk = window // dil
        nb = seq // window
        ii = lax.broadcasted_iota(jnp.int32, (wk, wk), 0)
        jj = lax.broadcasted_iota(jnp.int32, (wk, wk), 1)
        mask_prev = jj >= ii
        mask_own = jj <= ii
        ones = jnp.ones((wk, LANES), BF16)

        def rows(start):
            return pl.ds(start, wk, stride=dil) if dil > 1 else pl.ds(start, wk)

        def block(idx, carry):
            n, r = idx // dil, idx % dil
            own = rows(n * window + r)
            prev = rows(jnp.maximum(n - 1, 0) * window + r)
            q = q_ref[0, own, :].astype(BF16)
            k_own = k_ref[0, own, :].astype(BF16)
            k_prev = k_ref[0, prev, :].astype(BF16)
            v_own = jnp.concatenate([v_ref[0, own, :].astype(BF16), ones], axis=1)
            v_prev = jnp.concatenate([v_ref[0, prev, :].astype(BF16), ones], axis=1)
            s_own = lax.dot_general(q, k_own, contract_last, preferred_element_type=F32) * scale
            s_prev = lax.dot_general(q, k_prev, contract_last, preferred_element_type=F32) * scale
            no_prev = jnp.where(n > 0, 0.0, neg_inf)
            s_own = jnp.where(mask_own, s_own, neg_inf)
            s_prev = jnp.where(mask_prev, s_prev + no_prev, neg_inf)
            m = jnp.max(jnp.maximum(s_own, s_prev), axis=-1, keepdims=True)
            p_own = jnp.exp(s_own - m).astype(BF16)
            p_prev = jnp.exp(s_prev - m).astype(BF16)
            acc = (jnp.dot(p_own, v_own, preferred_element_type=F32)
                   + jnp.dot(p_prev, v_prev, preferred_element_type=F32))
            o, den = acc[:, :LANES], acc[:, LANES:]
            if first:
                m_acc[own, :] = jnp.broadcast_to(m, (wk, LANES))
                den_acc[own, :] = den
                o_acc[own, :] = o
            else:
                m_old = m_acc[own, :]
                m_new = jnp.maximum(m_old, m)
                a_old = jnp.exp(m_old - m_new)
                a_new = jnp.exp(m - m_new)
                m_acc[own, :] = m_new
                den_acc[own, :] = den_acc[own, :] * a_old + den * a_new
                o_acc[own, :] = o_acc[own, :] * a_old + o * a_new
            return carry

        lax.fori_loop(0, nb * dil, block, 0, unroll=4)

    for gi, (window, dil) in enumerate(DIL_GROUPS):
        @pl.when(g == N_GROUPS - 1 - gi)
        def _(gi=gi, window=window, dil=dil):
            run_group(gi == N_GROUPS - 1, window, dil)

    @pl.when(g == N_GROUPS - 1)
    def _():
        y = (o_acc[...] / den_acc[...]) * _silu(z_ref[0])
        y_ref[...] = y.astype(y_ref.dtype)


def _attn_prompt(u3, *, batch, seq, h_g, q_off, k_off, v_off, z_off):
    for window, _ in DIL_GROUPS:
        assert seq % window == 0
    kern = functools.partial(_attn_prompt_kernel, seq=seq, scale=float(LANES) ** -0.5)

    def in_spec(off):
        return pl.BlockSpec((1, seq, LANES),
                            lambda b, h, g: (off + (N_GROUPS - 1 - g) * h_g + h, b, 0))

    return pl.pallas_call(
        kern,
        out_shape=jax.ShapeDtypeStruct((batch * seq, h_g * LANES), BF16),
        grid=(batch, h_g, N_GROUPS),
        in_specs=[in_spec(q_off), in_spec(k_off), in_spec(v_off),
                  pl.BlockSpec((1, seq, LANES), lambda b, h, g: (z_off + h, b, 0))],
        out_specs=pl.BlockSpec((seq, LANES), lambda b, h, g: (b, h)),
        scratch_shapes=[pltpu.VMEM((seq, LANES), F32)] * 3,
        compiler_params=_cparams(("parallel", "parallel", "arbitrary")),
        name="attn_prompt",
    )(u3, u3, u3, u3)


def _attn_sample_kernel(uq_ref, c0_ref, c1_ref, c2_ref, y_ref, *, td, h_g):
    c_refs = (c0_ref, c1_ref, c2_ref)
    b = pl.program_id(0)
    rows_blk = uq_ref.shape[0]
    per_blk = rows_blk // td
    ro = (b % per_blk) * td
    scale = float(LANES) ** -0.5
    neg_inf = -jnp.inf
    contract_last = (((1,), (1,)), ((), ()))
    expand = jnp.ones((h_g, LANES), BF16)
    q_part, k_part, v_part, z_part = 0, N_GROUPS, 2 * N_GROUPS, 3 * N_GROUPS

    for i in range(td):
        r = ro + i
        m_run = den_run = y_run = None
        for g, (window, dil) in enumerate(DIL_GROUPS):
            wk = window // dil
            c_ref = c_refs[g]
            q_t = uq_ref[r, q_part + g]
            if dil == 1:
                k_c, v_c = c_ref[0, :, 0], c_ref[0, :, 1]
                new_js = tuple(range(i + 1))
            else:
                k_c, v_c = c_ref[0, :, i, 0], c_ref[0, :, i, 1]
                new_js = (i,)
            s = lax.dot_general(k_c.reshape(wk * h_g, LANES).astype(BF16), (q_t * scale).astype(BF16),
                                contract_last, preferred_element_type=F32)
            s = s.reshape(wk, h_g, h_g)
            kh = lax.broadcasted_iota(jnp.int32, (wk, h_g, h_g), 1)
            qh = lax.broadcasted_iota(jnp.int32, (wk, h_g, h_g), 2)
            valid = kh == qh
            if dil == 1:
                valid = valid & (lax.broadcasted_iota(jnp.int32, (wk, h_g, h_g), 0) >= i)
            s = jnp.where(valid, s, neg_inf)
            m = jnp.max(jnp.max(s, axis=0), axis=-1, keepdims=True)
            s_new = [jnp.sum(uq_ref[ro + jn, k_part + g] * q_t, axis=-1, keepdims=True) * scale
                     for jn in new_js]
            for sn in s_new:
                m = jnp.maximum(m, sn)
            if m_run is not None:
                m = jnp.maximum(m, m_run)
            p = jnp.exp(s - m[None])
            den = jnp.sum(jnp.sum(p, axis=0), axis=-1, keepdims=True)
            pb = jnp.dot(p.astype(BF16).reshape(wk * h_g, h_g), expand,
                         preferred_element_type=F32)
            y = jnp.sum(pb.reshape(wk, h_g, LANES) * v_c, axis=0)
            for jn, sn in zip(new_js, s_new):
                pn = jnp.exp(sn - m)
                den = den + pn
                y = y + pn * uq_ref[ro + jn, v_part + g]
            if m_run is not None:
                a_old = jnp.exp(m_run - m)
                den = den + den_run * a_old
                y = y + y_run * a_old
            m_run, den_run, y_run = m, den, y
        y_ref[r] = (y_run / den_run) * _silu(uq_ref[r, z_part])


def _attn_sample(uq, caches, *, dec_batch, td, h_g):
    rows_blk = 8
    assert rows_blk % td == 0 and (dec_batch * td) % rows_blk == 0
    per_blk = rows_blk // td
    parts = uq.shape[1]
    views, c_specs = [], []
    for (window, dil), c in zip(DIL_GROUPS, caches):
        assert c.shape[1] == window, "window buffers must hold a full window"
        wk = window // dil
        if dil == 1:
            views.append(c)
            c_specs.append(pl.BlockSpec((1, wk, 2, h_g, LANES), lambda b: (b, 0, 0, 0, 0)))
        else:
            assert td <= dil
            views.append(c.reshape(dec_batch, wk, dil, 2, h_g, LANES))
            c_specs.append(pl.BlockSpec((1, wk, td, 2, h_g, LANES), lambda b: (b, 0, 0, 0, 0, 0)))
    kern = functools.partial(_attn_sample_kernel, td=td, h_g=h_g)
    return pl.pallas_call(
        kern,
        out_shape=jax.ShapeDtypeStruct((dec_batch * td, h_g, LANES), F32),
        grid=(dec_batch,),
        in_specs=[pl.BlockSpec((rows_blk, parts, h_g, LANES), lambda b: (b // per_blk, 0, 0, 0))] + c_specs,
        out_specs=pl.BlockSpec((rows_blk, h_g, LANES), lambda b: (b // per_blk, 0, 0)),
        compiler_params=_cparams(("arbitrary",)),
        name="attn_sample",
    )(uq, *views)


def _merge_kernel(ya_ref, yb_ref, wa_ref, wb_ref, ga_ref, gb_ref, o_ref, *, nc, blocked_gates):
    pa = jnp.dot(ya_ref[...].astype(BF16), wa_ref[...], preferred_element_type=F32)
    pb = jnp.dot(yb_ref[...].astype(BF16), wb_ref[...], preferred_element_type=F32)
    for c in range(nc):
        ls = slice(c * LANES, (c + 1) * LANES)
        ga = ga_ref[c] if blocked_gates else ga_ref[:, ls]
        gb = gb_ref[c] if blocked_gates else gb_ref[:, ls]
        o_ref[:, ls] = (_sigmoid(ga) * pa[:, ls] + _sigmoid(gb) * pb[:, ls]).astype(o_ref.dtype)


def _merge(ya, yb, wa_bf, wb_bf, gates, *, ga_off, gb_off, blocked_gates):
    m = ya.shape[0]
    d = wa_bf.shape[1]
    tm = ROW_TILE
    tn = _largest_divisor(d, (1024, 512, 256, 128))
    nc = tn // LANES
    assert (ga_off * LANES) % tn == 0 and (gb_off * LANES) % tn == 0
    ga_blk, gb_blk = ga_off * LANES // tn, gb_off * LANES // tn
    if blocked_gates:
        ga_spec = pl.BlockSpec((nc, tm, LANES), lambda j, i: (ga_blk + j, i, 0))
        gb_spec = pl.BlockSpec((nc, tm, LANES), lambda j, i: (gb_blk + j, i, 0))
    else:
        ga_spec = pl.BlockSpec((tm, tn), lambda j, i: (i, ga_blk + j))
        gb_spec = pl.BlockSpec((tm, tn), lambda j, i: (i, gb_blk + j))
    kern = functools.partial(_merge_kernel, nc=nc, blocked_gates=blocked_gates)
    return pl.pallas_call(
        kern,
        out_shape=jax.ShapeDtypeStruct((m, d), BF16),
        grid=(d // tn, m // tm),
        in_specs=[pl.BlockSpec((tm, ya.shape[1]), lambda j, i: (i, 0)),
                  pl.BlockSpec((tm, yb.shape[1]), lambda j, i: (i, 0)),
                  pl.BlockSpec((wa_bf.shape[0], tn), lambda j, i: (0, j)),
                  pl.BlockSpec((wb_bf.shape[0], tn), lambda j, i: (0, j)),
                  ga_spec, gb_spec],
        out_specs=pl.BlockSpec((tm, tn), lambda j, i: (i, j)),
        compiler_params=_cparams(("parallel", "parallel")),
        name="merge_blocked" if blocked_gates else "merge_rows",
    )(ya, yb, wa_bf, wb_bf, gates, gates)


def _outproj_kernel(mg_ref, w_ref, x_ref, fw_ref, y_ref, *, nj, tn):
    j = pl.program_id(1)
    val = x_ref[...] + jnp.dot(mg_ref[...], w_ref[...], preferred_element_type=F32)
    for jj in range(nj):
        @pl.when(j == jj)
        def _(jj=jj):
            y_ref[:, jj * tn:(jj + 1) * tn] = val

    @pl.when(j == nj - 1)
    def _():
        full = y_ref[...]
        ms = jnp.mean(full * full, axis=-1, keepdims=True)
        y_ref[...] = full * lax.rsqrt(ms + RMS_EPS) * fw_ref[...]


def _outproj(merged, w_bf, x, fw):
    m, d = x.shape
    tm = ROW_TILE
    tn = _largest_divisor(d, (512, 256, 128))
    nj = d // tn
    kern = functools.partial(_outproj_kernel, nj=nj, tn=tn)
    return pl.pallas_call(
        kern,
        out_shape=jax.ShapeDtypeStruct((m, d), F32),
        grid=(m // tm, nj),
        in_specs=[pl.BlockSpec((tm, d), lambda i, j: (i, 0)),
                  pl.BlockSpec((d, tn), lambda i, j: (0, j)),
                  pl.BlockSpec((tm, tn), lambda i, j: (i, j)),
                  pl.BlockSpec((1, d), lambda i, j: (0, 0))],
        out_specs=pl.BlockSpec((tm, d), lambda i, j: (i, 0)),
        compiler_params=_cparams(("parallel", "arbitrary")),
        name="outproj_norm",
    )(merged, w_bf, x, fw.reshape(1, d))


def _kv_prompt(u3, *, g, batch, seq, h_g, k_off, v_off):
    window, _ = DIL_GROUPS[g]
    keep = min(window, seq)

    def heads(off):
        slab = u3[off + g * h_g: off + (g + 1) * h_g].reshape(h_g, batch, seq, LANES)
        return slab[:, :, seq - keep:, :]

    kv = jnp.stack([heads(k_off), heads(v_off)], axis=0)
    return jnp.transpose(kv, (2, 3, 0, 1, 4))[None]


def kernel(x_prompt, x_sample, cache_kv_w128, cache_kv_w512, cache_kv_w2048, state_hgrn,
           norm_w, w_in, lb_logits, hgrn_norm_w, w_proj_a, w_proj_b, w_out, final_norm_w):
    depth = norm_w.shape[0]
    assert depth == 1, "single-layer step"
    layer = 0
    batch, seq, d = x_prompt.shape
    dec_batch, td, _ = x_sample.shape
    w_a = w_proj_a.shape[1]
    w_bo = w_proj_b.shape[1]
    n_heads = w_a // LANES
    h_g = w_bo // LANES
    n_b = N_GROUPS * h_g
    n_d = d // LANES
    qb_off = 4 * n_heads
    kb_off, vb_off = qb_off + n_b, qb_off + 2 * n_b
    zb_off = qb_off + 3 * n_b
    ga_off = zb_off + h_g
    gb_off = ga_off + n_d
    assert (gb_off + n_d) * LANES == w_in.shape[2]

    xp = x_prompt.reshape(batch * seq, d)
    xs = x_sample.reshape(dec_batch * td, d)
    caches = (cache_kv_w128[layer], cache_kv_w512[layer], cache_kv_w2048[layer])

    w_in_bf = _cast_bf16(w_in[layer])
    wa_bf = _cast_bf16(w_proj_a[layer])
    wb_bf = _cast_bf16(w_proj_b[layer])
    wo_bf = _cast_bf16(w_out[layer])

    h_p = _norm_cast(xp, norm_w[layer])
    h_s = _norm_cast(xs, norm_w[layer])
    u3, us, uq = _inproj(h_p, h_s, w_in_bf, h_g=h_g, attn_off=qb_off, attn_parts=3 * N_GROUPS + 1)

    nw = hgrn_norm_w[layer].reshape(1, LANES)
    ya_p, s_p = _hgrn_prompt(u3, lb_logits, nw, batch=batch, seq=seq, n_heads=n_heads, layer=layer)
    ya_s, s_s = _hgrn_sample(us, state_hgrn[layer], lb_logits, nw,
                             dec_batch=dec_batch, td=td, n_heads=n_heads, layer=layer)

    yb_p = _attn_prompt(u3, batch=batch, seq=seq, h_g=h_g,
                        q_off=qb_off, k_off=kb_off, v_off=vb_off, z_off=zb_off)
    yb_s = _attn_sample(uq, caches, dec_batch=dec_batch, td=td, h_g=h_g)
    yb_s = yb_s.reshape(dec_batch * td, h_g * LANES)

    mg_p = _merge(ya_p, yb_p, wa_bf, wb_bf, u3, ga_off=ga_off, gb_off=gb_off, blocked_gates=True)
    mg_s = _merge(ya_s, yb_s, wa_bf, wb_bf, us, ga_off=ga_off, gb_off=gb_off, blocked_gates=False)
    y_p = _outproj(mg_p, wo_bf, xp, final_norm_w)
    y_s = _outproj(mg_s, wo_bf, xs, final_norm_w)

    kv_p = [_kv_prompt(u3, g=g, batch=batch, seq=seq, h_g=h_g, k_off=kb_off, v_off=vb_off)
            for g in range(N_GROUPS)]
    kv_s = [jnp.stack([uq[:, N_GROUPS + g], uq[:, 2 * N_GROUPS + g]], axis=1)
            .reshape(1, dec_batch, td, 2, h_g, LANES) for g in range(N_GROUPS)]

    return (y_p.reshape(batch, seq, d), y_s.reshape(dec_batch, td, d),
            kv_p[0], kv_p[1], kv_p[2], s_p[None],
            kv_s[0], kv_s[1], kv_s[2], s_s[None])
```

```python
import functools

import jax
import jax.numpy as jnp
from jax import lax
from jax.experimental import pallas as pl
from jax.experimental.pallas import tpu as pltpu

F32 = jnp.float32
BF16 = jnp.bfloat16

LANES = 128
RMS_EPS = 1e-6
HGRN_CHUNK = 64
DIL_GROUPS = ((128, 1), (512, 4), (2048, 16))
N_GROUPS = len(DIL_GROUPS)
ROW_TILE = 512
VMEM_LIMIT = 56 * 1024 * 1024


def _cparams(semantics, vmem=VMEM_LIMIT):
    return pltpu.CompilerParams(dimension_semantics=semantics, vmem_limit_bytes=vmem)


def _sigmoid(x):
    return 1.0 / (1.0 + jnp.exp(-x))


def _silu(x):
    return x * _sigmoid(x)


def _largest_divisor(n, candidates):
    for c in candidates:
        if n % c == 0:
            return c
    raise ValueError(f"no tile in {candidates} divides {n}")


def _norm_cast_kernel(x_ref, w_ref, o_ref):
    x = x_ref[...]
    ms = jnp.mean(x * x, axis=-1, keepdims=True)
    o_ref[...] = (x * lax.rsqrt(ms + RMS_EPS) * w_ref[...]).astype(o_ref.dtype)


def _norm_cast(x, w):
    m, d = x.shape
    tm = _largest_divisor(m, (ROW_TILE, 256, 128))
    return pl.pallas_call(
        _norm_cast_kernel,
        out_shape=jax.ShapeDtypeStruct((m, d), BF16),
        grid=(m // tm,),
        in_specs=[pl.BlockSpec((tm, d), lambda i: (i, 0)),
                  pl.BlockSpec((1, d), lambda i: (0, 0))],
        out_specs=pl.BlockSpec((tm, d), lambda i: (i, 0)),
        compiler_params=_cparams(("parallel",)),
        name="norm_cast",
    )(x, w.reshape(1, d))


def _cast_kernel(x_ref, o_ref):
    o_ref[...] = x_ref[...].astype(o_ref.dtype)


def _cast_bf16(w):
    r, c = w.shape
    tr = _largest_divisor(r, (512, 256, 128))
    tc = _largest_divisor(c, (2048, 1024, 512, 256, 128))
    return pl.pallas_call(
        _cast_kernel,
        out_shape=jax.ShapeDtypeStruct((r, c), BF16),
        grid=(r // tr, c // tc),
        in_specs=[pl.BlockSpec((tr, tc), lambda i, j: (i, j))],
        out_specs=pl.BlockSpec((tr, tc), lambda i, j: (i, j)),
        compiler_params=_cparams(("parallel", "parallel")),
        name="cast_bf16",
    )(w)


def _inproj_kernel(hp_ref, hs_ref, w_hbm, u3_ref, us_ref, uq_ref, *rest,
                   n_p, nc, h_g, j_lo, j_hi, kv_plan, tiles_per_seq, n_chunks):
    kv_refs = rest[:len(kv_plan)]
    wbf_ref, stage_ref, sem = rest[len(kv_plan):]
    j = pl.program_id(0)
    i = pl.program_id(1)
    _, kc, tn = stage_ref.shape
    cur = j % 2
    nxt = 1 - cur

    def chunk_copy(tile, c, slot):
        return pltpu.make_async_copy(
            w_hbm.at[pl.ds(pl.multiple_of(c * kc, kc), kc), pl.ds(pl.multiple_of(tile * tn, tn), tn)],
            stage_ref.at[slot], sem.at[slot])

    def cast_chunk(c, slot, buf):
        wbf_ref[buf, pl.ds(pl.multiple_of(c * kc, kc), kc), :] = stage_ref[slot].astype(BF16)

    @pl.when((j == 0) & (i == 0))
    def _():
        def first_tile(c, carry):
            cp = chunk_copy(0, c, c % 2)
            cp.start()
            cp.wait()
            cast_chunk(c, c % 2, 0)
            return carry
        lax.fori_loop(0, n_chunks, first_tile, 0)

    has_next = j + 1 < pl.num_programs(0)

    @pl.when(has_next & (i >= 1) & (i <= n_chunks))
    def _():
        chunk_copy(j + 1, i - 1, (i + 1) % 2).wait()

    @pl.when(has_next & (i < n_chunks))
    def _():
        chunk_copy(j + 1, i, i % 2).start()

    def cast_arrived_chunk():
        cast_chunk(jnp.clip(i - 1, 0, n_chunks - 1), (i + 1) % 2, nxt)

    @pl.when(i < n_p)
    def _():
        cast_arrived_chunk()
        acc = jnp.dot(hp_ref[...], wbf_ref[cur], preferred_element_type=F32)
        for c in range(nc):
            u3_ref[c] = acc[:, c * LANES:(c + 1) * LANES]

        tile_in_seq = i % tiles_per_seq
        for kv_ref, (jk, ck, jv, cv, n_keep_tiles) in zip(kv_refs, kv_plan):
            keep_rows = kv_ref.shape[0]
            for jw, c0 in ((jk, ck), (jv, cv)):
                @pl.when((j == jw) & (tile_in_seq >= tiles_per_seq - n_keep_tiles))
                def _(kv_ref=kv_ref, c0=c0, keep_rows=keep_rows):
                    r0 = acc.shape[0] - keep_rows
                    for h in range(h_g):
                        kv_ref[:, 0, h, :] = acc[r0:, (c0 + h) * LANES:(c0 + h + 1) * LANES]

    @pl.when(i >= n_p)
    def _():
        cast_arrived_chunk()
        acc = jnp.dot(hs_ref[...], wbf_ref[cur], preferred_element_type=F32)
        us_ref[...] = acc

        @pl.when((j >= j_lo) & (j < j_hi))
        def _():
            for c in range(nc):
                uq_ref[:, c // h_g, c % h_g, :] = acc[:, c * LANES:(c + 1) * LANES]


def _inproj(h_p, h_s, w_f32, *, batch, seq, h_g, attn_off, attn_parts):
    mp, d = h_p.shape
    ms = h_s.shape[0]
    n = w_f32.shape[1]
    tm = ROW_TILE
    tn = _largest_divisor(n, (1024, 512, 256, 128))
    n_p, n_s = mp // tm, ms // tm
    n_chunks = n_p + n_s - 1
    assert d % n_chunks == 0 and (d // n_chunks) % 16 == 0
    kc = d // n_chunks
    nc = tn // LANES
    assert nc % h_g == 0 and (attn_off * LANES) % tn == 0 and (attn_parts * h_g) % nc == 0
    ppt = nc // h_g
    j_lo = attn_off * LANES // tn
    n_aj = attn_parts // ppt
    assert seq % tm == 0
    tiles_per_seq = seq // tm

    kv_plan, kv_shapes, kv_specs = [], [], []
    for g, (window, _) in enumerate(DIL_GROUPS):
        keep = min(window, seq)
        blk_rows = min(keep, tm)
        assert keep % blk_rows == 0 and seq % keep == 0
        nkt = keep // blk_rows
        k_part, v_part = N_GROUPS + g, 2 * N_GROUPS + g
        jk, ck = j_lo + k_part // ppt, (k_part % ppt) * h_g
        jv, cv = j_lo + v_part // ppt, (v_part % ppt) * h_g
        assert jk < jv
        kv_plan.append((jk, ck, jv, cv, nkt))
        kv_shapes.append(jax.ShapeDtypeStruct((batch * keep, 2, h_g, LANES), F32))
        last = batch * nkt - 1

        def kv_index(j, i, jk=jk, jv=jv, nkt=nkt, last=last):
            ip = jnp.minimum(i, n_p - 1)
            moving = (ip // tiles_per_seq) * nkt + jnp.clip(ip % tiles_per_seq - (tiles_per_seq - nkt), 0, nkt - 1)
            rb = jnp.where(j < jk, 0, jnp.where((j == jk) | (j == jv), moving, last))
            return (rb, (j >= jv).astype(jnp.int32), 0, 0)

        kv_specs.append(pl.BlockSpec((blk_rows, 1, h_g, LANES), kv_index))

    kern = functools.partial(_inproj_kernel, n_p=n_p, nc=nc, h_g=h_g, j_lo=j_lo, j_hi=j_lo + n_aj,
                             kv_plan=tuple(kv_plan), tiles_per_seq=tiles_per_seq, n_chunks=n_chunks)
    return pl.pallas_call(
        kern,
        out_shape=(jax.ShapeDtypeStruct((n // LANES, mp, LANES), F32),
                   jax.ShapeDtypeStruct((ms, n), F32),
                   jax.ShapeDtypeStruct((ms, attn_parts, h_g, LANES), F32),
                   *kv_shapes),
        grid=(n // tn, n_p + n_s),
        in_specs=[
            pl.BlockSpec((tm, d), lambda j, i: (jnp.minimum(i, n_p - 1), 0)),
            pl.BlockSpec((tm, d), lambda j, i: (jnp.clip(i - n_p, 0, n_s - 1), 0),
                         pipeline_mode=pl.Buffered(1) if n_s == 1 else None),
            pl.BlockSpec(memory_space=pl.ANY),
        ],
        out_specs=(
            pl.BlockSpec((nc, tm, LANES), lambda j, i: (j, jnp.minimum(i, n_p - 1), 0)),
            pl.BlockSpec((tm, tn), lambda j, i: (jnp.clip(i - n_p, 0, n_s - 1), j)),
            pl.BlockSpec((tm, ppt, h_g, LANES),
                         lambda j, i: (jnp.clip(i - n_p, 0, n_s - 1), jnp.clip(j - j_lo, 0, n_aj - 1), 0, 0)),
            *kv_specs,
        ),
        scratch_shapes=[pltpu.VMEM((2, d, tn), BF16),
                        pltpu.VMEM((2, kc, tn), F32),
                        pltpu.SemaphoreType.DMA((2,))],
        compiler_params=_cparams(("arbitrary", "arbitrary")),
        name="inproj",
    )(h_p, h_s, w_f32)


def _lower_bound(lb_logits, layer):
    mx = jnp.max(lb_logits, axis=0, keepdims=True)
    e = jnp.exp(lb_logits - mx)
    den = jnp.sum(e, axis=0, keepdims=True)
    num = e[0:1]
    for r in range(1, layer + 1):
        num = num + e[r:r + 1]
    return num / den


def _split3_bf16(x):
    hi = x.astype(BF16)
    r1 = x - hi.astype(F32)
    mid = r1.astype(BF16)
    lo = (r1 - mid.astype(F32)).astype(BF16)
    return hi, mid, lo


def _hgrn_prompt_kernel(q_ref, f_ref, i_ref, z_ref, lb_ref, nw_ref, y_ref, sfin_ref, st_ref,
                        *, hb, tb, chunk, cpb, layer):
    t = pl.program_id(2)

    @pl.when(t == 0)
    def _():
        st_ref[...] = jnp.zeros_like(st_ref)

    lb_all = _lower_bound(lb_ref[...], layer)
    nw = nw_ref[...]
    rb = cpb * chunk
    row = lax.broadcasted_iota(jnp.int32, (rb, rb), 0)
    col = lax.broadcasted_iota(jnp.int32, (rb, rb), 1)
    causal = (row >= col) & (row // chunk == col // chunk)
    tril = jnp.where(causal, 1.0, 0.0).astype(BF16)
    dk = q_ref.shape[-1]
    qscale = float(dk) ** -0.5
    contract_last = (((1,), (1,)), ((), ()))
    contract_first = (((0,), (0,)), ((), ()))

    def block_body(blk, carry):
        r0 = pl.multiple_of(blk * rb, rb)
        for hl in range(hb):
            lb = lb_all[:, hl * LANES:(hl + 1) * LANES]
            qa = q_ref[hl, pl.ds(r0, rb), :]
            fa = f_ref[hl, pl.ds(r0, rb), :]
            v = i_ref[hl, pl.ds(r0, rb), :]
            za = z_ref[hl, pl.ds(r0, rb), :]
            q = _silu(qa) * qscale
            f = lb + (1.0 - lb) * _sigmoid(fa)
            log_f = jnp.log(f)
            k = 1.0 - f
            hi, mid, lo = _split3_bf16(log_f)
            parts = jnp.dot(tril, jnp.concatenate([hi, mid, lo], axis=1),
                            preferred_element_type=F32)
            g = (parts[:, 0:LANES] + parts[:, LANES:2 * LANES]) + parts[:, 2 * LANES:3 * LANES]
            g_last = [g[(c + 1) * chunk - 1:(c + 1) * chunk, :] for c in range(cpb)]
            g_last_rows = jnp.concatenate(
                [jnp.broadcast_to(gl, (chunk, LANES)) for gl in g_last], axis=0)
            q_dec = (q * jnp.exp(g)).astype(BF16)
            k_rel = (k * jnp.exp(-g)).astype(BF16)
            k_end = (k * jnp.exp(g_last_rows - g)).astype(BF16)
            vb = v.astype(BF16)
            a = lax.dot_general(q_dec, k_rel, contract_last, preferred_element_type=F32)
            a = jnp.where(causal, a, 0.0)
            o_intra = jnp.dot(a.astype(BF16), vb, preferred_element_type=F32)
            st = st_ref[hl]
            o_inter = []
            for c in range(cpb):
                rows = slice(c * chunk, (c + 1) * chunk)
                o_inter.append(lax.dot_general(q_dec[rows], st.astype(BF16), contract_last,
                                               preferred_element_type=F32))
                ds_t = lax.dot_general(vb[rows], k_end[rows], contract_first,
                                       preferred_element_type=F32)
                st = jnp.exp(g_last[c]) * st + ds_t
            st_ref[hl] = st
            o = o_intra + jnp.concatenate(o_inter, axis=0)
            ms = jnp.mean(o * o, axis=-1, keepdims=True)
            y = (o * lax.rsqrt(ms + RMS_EPS) * nw) * _silu(za)
            y_ref[pl.ds(r0, rb), hl * LANES:(hl + 1) * LANES] = y.astype(y_ref.dtype)
        return carry

    lax.fori_loop(0, tb // rb, block_body, 0)

    @pl.when(t == pl.num_programs(2) - 1)
    def _():
        for hl in range(hb):
            sfin_ref[0, hl] = st_ref[hl].T


def _hgrn_prompt(u3, lb_logits, nw, *, batch, seq, n_heads, layer):
    hb = _largest_divisor(n_heads, (4, 2, 1))
    tb = _largest_divisor(seq, (1024, 512, 256, 128, 64))
    chunk = HGRN_CHUNK
    assert tb % chunk == 0
    nt = seq // tb
    nhb = n_heads // hb
    cpb = _largest_divisor(tb // chunk, (4, 2, 1))
    kern = functools.partial(_hgrn_prompt_kernel, hb=hb, tb=tb, chunk=chunk, cpb=cpb, layer=layer)

    def in_spec(part):
        return pl.BlockSpec((hb, tb, LANES), lambda b, h, t: (part * nhb + h, b * nt + t, 0))

    return pl.pallas_call(
        kern,
        out_shape=(jax.ShapeDtypeStruct((batch * seq, n_heads * LANES), BF16),
                   jax.ShapeDtypeStruct((batch, n_heads, LANES, LANES), F32)),
        grid=(batch, nhb, nt),
        in_specs=[in_spec(0), in_spec(1), in_spec(2), in_spec(3),
                  pl.BlockSpec((lb_logits.shape[0], hb * LANES), lambda b, h, t: (0, h)),
                  pl.BlockSpec((1, LANES), lambda b, h, t: (0, 0))],
        out_specs=(pl.BlockSpec((tb, hb * LANES), lambda b, h, t: (b * nt + t, h)),
                   pl.BlockSpec((1, hb, LANES, LANES), lambda b, h, t: (b, h, 0, 0))),
        scratch_shapes=[pltpu.VMEM((hb, LANES, LANES), F32)],
        compiler_params=_cparams(("parallel", "parallel", "arbitrary")),
        name="hgrn_prompt",
    )(u3, u3, u3, u3, lb_logits, nw)


def _hgrn_sample_kernel(q_ref, f_ref, i_ref, z_ref, lb_ref, nw_ref, s_ref, y_ref, so_ref, o_scr,
                        *, bb, td, layer):
    lb = _lower_bound(lb_ref[...], layer)
    dk = q_ref.shape[-1]
    q = _silu(q_ref[...]) * (float(dk) ** -0.5)
    f = lb + (1.0 - lb) * _sigmoid(f_ref[...])
    k = 1.0 - f
    v = i_ref[...]
    rows = bb * td
    sub = 8
    contract_last = (((1,), (1,)), ((), ()))
    contract_first = (((0,), (0,)), ((), ()))
    t_idx = lax.broadcasted_iota(jnp.int32, (rows, LANES), 0) % td
    p_inc = f
    s = 1
    while s < td:
        p_inc = p_inc * jnp.where(t_idx >= s, pltpu.roll(p_inc, s, axis=0), 1.0)
        s *= 2
    e_suf = jnp.where(t_idx + 1 < td, pltpu.roll(f, rows - 1, axis=0), 1.0)
    s = 1
    while s < td:
        e_suf = e_suf * jnp.where(t_idx + s < td, pltpu.roll(e_suf, rows - s, axis=0), 1.0)
        s *= 2
    q_dec = q * p_inc
    k_end = k * e_suf
    a = lax.dot_general(q_dec.astype(BF16), (k / p_inc).astype(BF16), contract_last,
                        preferred_element_type=F32)
    ri = lax.broadcasted_iota(jnp.int32, (rows, rows), 0)
    ci = lax.broadcasted_iota(jnp.int32, (rows, rows), 1)
    a = jnp.where((ri // td == ci // td) & (ci <= ri), a, 0.0)
    o_intra = jnp.dot(a.astype(BF16), v.astype(BF16), preferred_element_type=F32)
    dec_t = p_inc.T
    row8 = lax.broadcasted_iota(jnp.int32, (sub, LANES), 0)
    for grp in range(rows // sub):
        sl = slice(grp * sub, (grp + 1) * sub)
        qd8 = q_dec[sl].astype(BF16)
        o8 = jnp.zeros((sub, LANES), F32)
        for u in range(sub // td):
            bl = grp * (sub // td) + u
            mine = (row8 >= u * td) & (row8 < (u + 1) * td)
            s0 = s_ref[bl, 0]
            o8 = jnp.where(mine, jnp.dot(qd8, s0.astype(BF16), preferred_element_type=F32), o8)
            ds = lax.dot_general(jnp.where(mine, k_end[sl], 0.0).astype(BF16),
                                 jnp.where(mine, v[sl], 0.0).astype(BF16), contract_first,
                                 preferred_element_type=F32)
            j = bl * td + td - 1
            so_ref[bl, 0] = dec_t[:, j:j + 1] * s0 + ds
        o_scr[sl, :] = o8
    o = o_intra + o_scr[...]
    ms = jnp.mean(o * o, axis=-1, keepdims=True)
    y = (o * lax.rsqrt(ms + RMS_EPS) * nw_ref[...]) * _silu(z_ref[...])
    y_ref[...] = y.astype(y_ref.dtype)


def _hgrn_sample(us, state, lb_logits, nw, *, dec_batch, td, n_heads, layer):
    rows = LANES
    assert rows % td == 0 and 8 % td == 0
    bb = rows // td
    assert dec_batch % bb == 0
    kern = functools.partial(_hgrn_sample_kernel, bb=bb, td=td, layer=layer)

    def in_spec(part):
        return pl.BlockSpec((rows, LANES), lambda bi, h: (bi, part * n_heads + h))

    return pl.pallas_call(
        kern,
        out_shape=(jax.ShapeDtypeStruct((dec_batch * td, n_heads * LANES), BF16),
                   jax.ShapeDtypeStruct(state.shape, F32)),
        grid=(dec_batch // bb, n_heads),
        in_specs=[in_spec(0), in_spec(1), in_spec(2), in_spec(3),
                  pl.BlockSpec((lb_logits.shape[0], LANES), lambda bi, h: (0, h)),
                  pl.BlockSpec((1, LANES), lambda bi, h: (0, 0)),
                  pl.BlockSpec((bb, 1, LANES, LANES), lambda bi, h: (bi, h, 0, 0))],
        out_specs=(pl.BlockSpec((rows, LANES), lambda bi, h: (bi, h)),
                   pl.BlockSpec((bb, 1, LANES, LANES), lambda bi, h: (bi, h, 0, 0))),
        scratch_shapes=[pltpu.VMEM((rows, LANES), F32)],
        compiler_params=_cparams(("parallel", "parallel")),
        name="hgrn_sample",
    )(us, us, us, us, lb_logits, nw, state)


def _attn_prompt_kernel(q_ref, k_ref, v_ref, z_ref, y_ref, m_acc, den_acc, o_acc, *, seq, scale):
    g = pl.program_id(2)
    neg_inf = -jnp.inf
    contract_last = (((1,), (1,)), ((), ()))

    def run_group(first, window, dil):
        wk = window // dil
        nb = seq // window
        ii = lax.broadcasted_iota(jnp.int32, (wk, wk), 0)
        jj = lax.broadcasted_iota(jnp.int32, (wk, wk), 1)
        mask_prev = jj >= ii
        mask_own = jj <= ii
        ones = jnp.ones((wk, LANES), BF16)

        def rows(start):
            return pl.ds(start, wk, stride=dil) if dil > 1 else pl.ds(start, wk)

        def block(idx, carry):
            n, r = idx // dil, idx % dil
            own = rows(n * window + r)
            prev = rows(jnp.maximum(n - 1, 0) * window + r)
            q = q_ref[0, own, :].astype(BF16)
            k_own = k_ref[0, own, :].astype(BF16)
            k_prev = k_ref[0, prev, :].astype(BF16)
            v_own = jnp.concatenate([v_ref[0, own, :].astype(BF16), ones], axis=1)
            v_prev = jnp.concatenate([v_ref[0, prev, :].astype(BF16), ones], axis=1)
            s_own = lax.dot_general(q, k_own, contract_last, preferred_element_type=F32) * scale
            s_prev = lax.dot_general(q, k_prev, contract_last, preferred_element_type=F32) * scale
            no_prev = jnp.where(n > 0, 0.0, neg_inf)
            s_own = jnp.where(mask_own, s_own, neg_inf)
            s_prev = jnp.where(mask_prev, s_prev + no_prev, neg_inf)
            m = jnp.max(jnp.maximum(s_own, s_prev), axis=-1, keepdims=True)
            p_own = jnp.exp(s_own - m).astype(BF16)
            p_prev = jnp.exp(s_prev - m).astype(BF16)
            acc = (jnp.dot(p_own, v_own, preferred_element_type=F32)
                   + jnp.dot(p_prev, v_prev, preferred_element_type=F32))
            o, den = acc[:, :LANES], acc[:, LANES:]
            if first:
                m_acc[own, :] = jnp.broadcast_to(m, (wk, LANES))
                den_acc[own, :] = den
                o_acc[own, :] = o
            else:
                m_old = m_acc[own, :]
                m_new = jnp.maximum(m_old, m)
                a_old = jnp.exp(m_old - m_new)
                a_new = jnp.exp(m - m_new)
                m_acc[own, :] = m_new
                den_acc[own, :] = den_acc[own, :] * a_old + den * a_new
                o_acc[own, :] = o_acc[own, :] * a_old + o * a_new
            return carry

        lax.fori_loop(0, nb * dil, block, 0, unroll=4)

    for gi, (window, dil) in enumerate(DIL_GROUPS):
        @pl.when(g == N_GROUPS - 1 - gi)
        def _(gi=gi, window=window, dil=dil):
            run_group(gi == N_GROUPS - 1, window, dil)

    @pl.when(g == N_GROUPS - 1)
    def _():
        y = (o_acc[...] / den_acc[...]) * _silu(z_ref[0])
        y_ref[...] = y.astype(y_ref.dtype)


def _attn_prompt(u3, *, batch, seq, h_g, q_off, k_off, v_off, z_off):
    for window, _ in DIL_GROUPS:
        assert seq % window == 0
    kern = functools.partial(_attn_prompt_kernel, seq=seq, scale=float(LANES) ** -0.5)

    def in_spec(off):
        return pl.BlockSpec((1, seq, LANES),
                            lambda b, h, g: (off + (N_GROUPS - 1 - g) * h_g + h, b, 0))

    return pl.pallas_call(
        kern,
        out_shape=jax.ShapeDtypeStruct((batch * seq, h_g * LANES), BF16),
        grid=(batch, h_g, N_GROUPS),
        in_specs=[in_spec(q_off), in_spec(k_off), in_spec(v_off),
                  pl.BlockSpec((1, seq, LANES), lambda b, h, g: (z_off + h, b, 0))],
        out_specs=pl.BlockSpec((seq, LANES), lambda b, h, g: (b, h)),
        scratch_shapes=[pltpu.VMEM((seq, LANES), F32)] * 3,
        compiler_params=_cparams(("parallel", "parallel", "arbitrary")),
        name="attn_prompt",
    )(u3, u3, u3, u3)


def _attn_sample_kernel(uq_ref, c0_ref, c1_ref, c2_ref, y_ref, *, td, h_g):
    c_refs = (c0_ref, c1_ref, c2_ref)
    b = pl.program_id(0)
    rows_blk = uq_ref.shape[0]
    per_blk = rows_blk // td
    ro = (b % per_blk) * td
    scale = float(LANES) ** -0.5
    neg_inf = -jnp.inf
    contract_last = (((1,), (1,)), ((), ()))
    q_part, k_part, v_part, z_part = 0, N_GROUPS, 2 * N_GROUPS, 3 * N_GROUPS
    wk = DIL_GROUPS[0][0] // DIL_GROUPS[0][1]
    nl = N_GROUPS * h_g
    shape3 = (wk, h_g, nl)
    c_row = lax.broadcasted_iota(jnp.int32, shape3, 0)
    key_head = lax.broadcasted_iota(jnp.int32, shape3, 1)
    lane = lax.broadcasted_iota(jnp.int32, shape3, 2)
    head_match = lane % h_g == key_head
    e_row = lax.broadcasted_iota(jnp.int32, (nl, LANES), 0)
    expand = [jnp.where(e_row // h_g == g, 1.0, 0.0).astype(BF16) for g in range(N_GROUPS)]
    zeros_q = jnp.zeros((h_g, LANES), F32)

    for i in range(td):
        r = ro + i
        valid = head_match
        s = None
        v_cs, s_new, v_new = [], [], []
        for g, (window, dil) in enumerate(DIL_GROUPS):
            assert window // dil == wk
            c_ref = c_refs[g]
            q_t = uq_ref[r, q_part + g]
            if dil == 1:
                k_c, v_c = c_ref[0, :, 0], c_ref[0, :, 1]
                new_js = tuple(range(i + 1))
                valid = valid & ((lane // h_g != g) | (c_row >= i))
            else:
                k_c, v_c = c_ref[0, :, i, 0], c_ref[0, :, i, 1]
                new_js = (i,)
            q_pad = jnp.concatenate([q_t * scale if gg == g else zeros_q for gg in range(N_GROUPS)],
                                    axis=0).astype(BF16)
            s_g = lax.dot_general(k_c.reshape(wk * h_g, LANES).astype(BF16), q_pad,
                                  contract_last, preferred_element_type=F32)
            s = s_g if s is None else s + s_g
            v_cs.append(v_c)
            for jn in new_js:
                s_new.append(jnp.sum(uq_ref[ro + jn, k_part + g] * q_t, axis=-1, keepdims=True) * scale)
                v_new.append(uq_ref[ro + jn, v_part + g])
        s = jnp.where(valid, s.reshape(shape3), neg_inf)
        m = jnp.max(jnp.max(s, axis=0), axis=-1, keepdims=True)
        for sn in s_new:
            m = jnp.maximum(m, sn)
        p = jnp.exp(s - m[None])
        den = jnp.sum(jnp.sum(p, axis=0), axis=-1, keepdims=True)
        p_bf = p.astype(BF16).reshape(wk * h_g, nl)
        y = jnp.zeros((h_g, LANES), F32)
        for g in range(N_GROUPS):
            pb = jnp.dot(p_bf, expand[g], preferred_element_type=F32)
            y = y + jnp.sum(pb.reshape(wk, h_g, LANES) * v_cs[g], axis=0)
        for sn, vn in zip(s_new, v_new):
            pn = jnp.exp(sn - m)
            den = den + pn
            y = y + pn * vn
        y_ref[r] = (y / den) * _silu(uq_ref[r, z_part])


def _attn_sample(uq, caches, *, dec_batch, td, h_g):
    rows_blk = 8
    assert rows_blk % td == 0 and (dec_batch * td) % rows_blk == 0
    per_blk = rows_blk // td
    parts = uq.shape[1]
    views, c_specs = [], []
    for (window, dil), c in zip(DIL_GROUPS, caches):
        assert c.shape[1] == window, "window buffers must hold a full window"
        wk = window // dil
        if dil == 1:
            views.append(c)
            c_specs.append(pl.BlockSpec((1, wk, 2, h_g, LANES), lambda b: (b, 0, 0, 0, 0)))
        else:
            assert td <= dil
            views.append(c.reshape(dec_batch, wk, dil, 2, h_g, LANES))
            c_specs.append(pl.BlockSpec((1, wk, td, 2, h_g, LANES), lambda b: (b, 0, 0, 0, 0, 0)))
    kern = functools.partial(_attn_sample_kernel, td=td, h_g=h_g)
    return pl.pallas_call(
        kern,
        out_shape=jax.ShapeDtypeStruct((dec_batch * td, h_g, LANES), F32),
        grid=(dec_batch,),
        in_specs=[pl.BlockSpec((rows_blk, parts, h_g, LANES), lambda b: (b // per_blk, 0, 0, 0))] + c_specs,
        out_specs=pl.BlockSpec((rows_blk, h_g, LANES), lambda b: (b // per_blk, 0, 0)),
        compiler_params=_cparams(("arbitrary",)),
        name="attn_sample",
    )(uq, *views)


def _merge_kernel(ya_ref, yb_ref, wa_ref, wb_ref, ga_ref, gb_ref, o_ref, *, nc, blocked_gates):
    pa = jnp.dot(ya_ref[...].astype(BF16), wa_ref[...], preferred_element_type=F32)
    pb = jnp.dot(yb_ref[...].astype(BF16), wb_ref[...], preferred_element_type=F32)
    for c in range(nc):
        ls = slice(c * LANES, (c + 1) * LANES)
        ga = ga_ref[c] if blocked_gates else ga_ref[:, ls]
        gb = gb_ref[c] if blocked_gates else gb_ref[:, ls]
        o_ref[:, ls] = (_sigmoid(ga) * pa[:, ls] + _sigmoid(gb) * pb[:, ls]).astype(o_ref.dtype)


def _merge(ya, yb, wa_bf, wb_bf, gates, *, ga_off, gb_off, blocked_gates):
    m = ya.shape[0]
    d = wa_bf.shape[1]
    tm = ROW_TILE
    tn = _largest_divisor(d, (1024, 512, 256, 128))
    nc = tn // LANES
    assert (ga_off * LANES) % tn == 0 and (gb_off * LANES) % tn == 0
    ga_blk, gb_blk = ga_off * LANES // tn, gb_off * LANES // tn
    if blocked_gates:
        ga_spec = pl.BlockSpec((nc, tm, LANES), lambda j, i: (ga_blk + j, i, 0))
        gb_spec = pl.BlockSpec((nc, tm, LANES), lambda j, i: (gb_blk + j, i, 0))
    else:
        ga_spec = pl.BlockSpec((tm, tn), lambda j, i: (i, ga_blk + j))
        gb_spec = pl.BlockSpec((tm, tn), lambda j, i: (i, gb_blk + j))
    kern = functools.partial(_merge_kernel, nc=nc, blocked_gates=blocked_gates)
    return pl.pallas_call(
        kern,
        out_shape=jax.ShapeDtypeStruct((m, d), BF16),
        grid=(d // tn, m // tm),
        in_specs=[pl.BlockSpec((tm, ya.shape[1]), lambda j, i: (i, 0)),
                  pl.BlockSpec((tm, yb.shape[1]), lambda j, i: (i, 0)),
                  pl.BlockSpec((wa_bf.shape[0], tn), lambda j, i: (0, j)),
                  pl.BlockSpec((wb_bf.shape[0], tn), lambda j, i: (0, j)),
                  ga_spec, gb_spec],
        out_specs=pl.BlockSpec((tm, tn), lambda j, i: (i, j)),
        compiler_params=_cparams(("parallel", "parallel")),
        name="merge_blocked" if blocked_gates else "merge_rows",
    )(ya, yb, wa_bf, wb_bf, gates, gates)


def _outproj_kernel(mg_ref, w_ref, x_ref, fw_ref, y_ref, *, nj, tn):
    j = pl.program_id(1)
    val = x_ref[...] + jnp.dot(mg_ref[...], w_ref[...], preferred_element_type=F32)
    for jj in range(nj):
        @pl.when(j == jj)
        def _(jj=jj):
            y_ref[:, jj * tn:(jj + 1) * tn] = val

    @pl.when(j == nj - 1)
    def _():
        rc = 64
        fw = fw_ref[...]

        def norm_rows(c, carry):
            rows = pl.ds(pl.multiple_of(c * rc, rc), rc)
            full = y_ref[rows, :]
            ms = jnp.mean(full * full, axis=-1, keepdims=True)
            y_ref[rows, :] = full * lax.rsqrt(ms + RMS_EPS) * fw
            return carry

        lax.fori_loop(0, y_ref.shape[0] // rc, norm_rows, 0)


def _outproj(merged, w_bf, x, fw):
    m, d = x.shape
    tm = ROW_TILE
    tn = _largest_divisor(d, (1024, 512, 256, 128))
    nj = d // tn
    kern = functools.partial(_outproj_kernel, nj=nj, tn=tn)
    return pl.pallas_call(
        kern,
        out_shape=jax.ShapeDtypeStruct((m, d), F32),
        grid=(m // tm, nj),
        in_specs=[pl.BlockSpec((tm, d), lambda i, j: (i, 0)),
                  pl.BlockSpec((d, tn), lambda i, j: (0, j)),
                  pl.BlockSpec((tm, tn), lambda i, j: (i, j)),
                  pl.BlockSpec((1, d), lambda i, j: (0, 0))],
        out_specs=pl.BlockSpec((tm, d), lambda i, j: (i, 0)),
        compiler_params=_cparams(("parallel", "arbitrary")),
        name="outproj_norm",
    )(merged, w_bf, x, fw.reshape(1, d))


def kernel(x_prompt, x_sample, cache_kv_w128, cache_kv_w512, cache_kv_w2048, state_hgrn,
           norm_w, w_in, lb_logits, hgrn_norm_w, w_proj_a, w_proj_b, w_out, final_norm_w):
    depth = norm_w.shape[0]
    assert depth == 1, "single-layer step"
    layer = 0
    batch, seq, d = x_prompt.shape
    dec_batch, td, _ = x_sample.shape
    w_a = w_proj_a.shape[1]
    w_bo = w_proj_b.shape[1]
    n_heads = w_a // LANES
    h_g = w_bo // LANES
    n_b = N_GROUPS * h_g
    n_d = d // LANES
    qb_off = 4 * n_heads
    kb_off, vb_off = qb_off + n_b, qb_off + 2 * n_b
    zb_off = qb_off + 3 * n_b
    ga_off = zb_off + h_g
    gb_off = ga_off + n_d
    assert (gb_off + n_d) * LANES == w_in.shape[2]

    xp = x_prompt.reshape(batch * seq, d)
    xs = x_sample.reshape(dec_batch * td, d)
    caches = (cache_kv_w128[layer], cache_kv_w512[layer], cache_kv_w2048[layer])

    wa_bf = _cast_bf16(w_proj_a[layer])
    wb_bf = _cast_bf16(w_proj_b[layer])
    wo_bf = _cast_bf16(w_out[layer])

    h_p = _norm_cast(xp, norm_w[layer])
    h_s = _norm_cast(xs, norm_w[layer])
    u3, us, uq, *kv_p = _inproj(h_p, h_s, w_in[layer], batch=batch, seq=seq, h_g=h_g,
                                attn_off=qb_off, attn_parts=3 * N_GROUPS + 1)
    kv_p = [kv.reshape(1, batch, kv.shape[0] // batch, 2, h_g, LANES) for kv in kv_p]

    nw = hgrn_norm_w[layer].reshape(1, LANES)
    ya_p, s_p = _hgrn_prompt(u3, lb_logits, nw, batch=batch, seq=seq, n_heads=n_heads, layer=layer)
    ya_s, s_s = _hgrn_sample(us, state_hgrn[layer], lb_logits, nw,
                             dec_batch=dec_batch, td=td, n_heads=n_heads, layer=layer)

    yb_p = _attn_prompt(u3, batch=batch, seq=seq, h_g=h_g,
                        q_off=qb_off, k_off=kb_off, v_off=vb_off, z_off=zb_off)
    yb_s = _attn_sample(uq, caches, dec_batch=dec_batch, td=td, h_g=h_g)
    yb_s = yb_s.reshape(dec_batch * td, h_g * LANES)

    mg_p = _merge(ya_p, yb_p, wa_bf, wb_bf, u3, ga_off=ga_off, gb_off=gb_off, blocked_gates=True)
    mg_s = _merge(ya_s, yb_s, wa_bf, wb_bf, us, ga_off=ga_off, gb_off=gb_off, blocked_gates=False)
    y_p = _outproj(mg_p, wo_bf, xp, final_norm_w)
    y_s = _outproj(mg_s, wo_bf, xs, final_norm_w)

    kv_s = [jnp.stack([uq[:, N_GROUPS + g], uq[:, 2 * N_GROUPS + g]], axis=1)
            .reshape(1, dec_batch, td, 2, h_g, LANES) for g in range(N_GROUPS)]

    return (y_p.reshape(batch, seq, d), y_s.reshape(dec_batch, td, d),
            kv_p[0], kv_p[1], kv_p[2], s_p[None],
            kv_s[0], kv_s[1], kv_s[2], s_s[None])
```

```python
import functools

import jax
import jax.numpy as jnp
from jax import lax
from jax.experimental import pallas as pl
from jax.experimental.pallas import tpu as pltpu

F32 = jnp.float32
BF16 = jnp.bfloat16

LANES = 128
RMS_EPS = 1e-6
HGRN_CHUNK = 64
DIL_GROUPS = ((128, 1), (512, 4), (2048, 16))
N_GROUPS = len(DIL_GROUPS)
ROW_TILE = 512
VMEM_LIMIT = 56 * 1024 * 1024


def _cparams(semantics, vmem=VMEM_LIMIT):
    return pltpu.CompilerParams(dimension_semantics=semantics, vmem_limit_bytes=vmem)


def _sigmoid(x):
    return 1.0 / (1.0 + jnp.exp(-x))


def _silu(x):
    return x * _sigmoid(x)


def _largest_divisor(n, candidates):
    for c in candidates:
        if n % c == 0:
            return c
    raise ValueError(f"no tile in {candidates} divides {n}")


def _norm_cast_kernel(x_ref, w_ref, o_ref):
    x = x_ref[...]
    ms = jnp.mean(x * x, axis=-1, keepdims=True)
    o_ref[...] = (x * lax.rsqrt(ms + RMS_EPS) * w_ref[...]).astype(o_ref.dtype)


def _norm_cast(x, w):
    m, d = x.shape
    tm = _largest_divisor(m, (ROW_TILE, 256, 128))
    return pl.pallas_call(
        _norm_cast_kernel,
        out_shape=jax.ShapeDtypeStruct((m, d), BF16),
        grid=(m // tm,),
        in_specs=[pl.BlockSpec((tm, d), lambda i: (i, 0)),
                  pl.BlockSpec((1, d), lambda i: (0, 0))],
        out_specs=pl.BlockSpec((tm, d), lambda i: (i, 0)),
        compiler_params=_cparams(("parallel",)),
        name="norm_cast",
    )(x, w.reshape(1, d))


def _cast_kernel(x_ref, o_ref):
    o_ref[...] = x_ref[...].astype(o_ref.dtype)


def _cast_bf16(w):
    r, c = w.shape
    tr = _largest_divisor(r, (512, 256, 128))
    tc = _largest_divisor(c, (2048, 1024, 512, 256, 128))
    return pl.pallas_call(
        _cast_kernel,
        out_shape=jax.ShapeDtypeStruct((r, c), BF16),
        grid=(r // tr, c // tc),
        in_specs=[pl.BlockSpec((tr, tc), lambda i, j: (i, j))],
        out_specs=pl.BlockSpec((tr, tc), lambda i, j: (i, j)),
        compiler_params=_cparams(("parallel", "parallel")),
        name="cast_bf16",
    )(w)


def _inproj_kernel(hp_ref, hs_ref, w_hbm, u3_ref, us_ref, uq_ref, *rest,
                   n_p, nc, h_g, j_lo, j_hi, kv_plan, tiles_per_seq, n_chunks):
    kv_refs = rest[:len(kv_plan)]
    wbf_ref, stage_ref, sem = rest[len(kv_plan):]
    j = pl.program_id(0)
    i = pl.program_id(1)
    _, kc, tn = stage_ref.shape
    cur = j % 2
    nxt = 1 - cur

    def chunk_copy(tile, c, slot):
        return pltpu.make_async_copy(
            w_hbm.at[pl.ds(pl.multiple_of(c * kc, kc), kc), pl.ds(pl.multiple_of(tile * tn, tn), tn)],
            stage_ref.at[slot], sem.at[slot])

    def cast_chunk(c, slot, buf):
        wbf_ref[buf, pl.ds(pl.multiple_of(c * kc, kc), kc), :] = stage_ref[slot].astype(BF16)

    @pl.when((j == 0) & (i == 0))
    def _():
        def first_tile(c, carry):
            cp = chunk_copy(0, c, c % 2)
            cp.start()
            cp.wait()
            cast_chunk(c, c % 2, 0)
            return carry
        lax.fori_loop(0, n_chunks, first_tile, 0)

    has_next = j + 1 < pl.num_programs(0)

    @pl.when(has_next & (i >= 1) & (i <= n_chunks))
    def _():
        chunk_copy(j + 1, i - 1, (i + 1) % 2).wait()

    @pl.when(has_next & (i < n_chunks))
    def _():
        chunk_copy(j + 1, i, i % 2).start()

    def cast_arrived_chunk():
        cast_chunk(jnp.clip(i - 1, 0, n_chunks - 1), (i + 1) % 2, nxt)

    @pl.when(i < n_p)
    def _():
        cast_arrived_chunk()
        acc = jnp.dot(hp_ref[...], wbf_ref[cur], preferred_element_type=F32)
        for c in range(nc):
            u3_ref[c] = acc[:, c * LANES:(c + 1) * LANES]

        tile_in_seq = i % tiles_per_seq
        for kv_ref, (jk, ck, jv, cv, n_keep_tiles) in zip(kv_refs, kv_plan):
            keep_rows = kv_ref.shape[0]
            for jw, c0 in ((jk, ck), (jv, cv)):
                @pl.when((j == jw) & (tile_in_seq >= tiles_per_seq - n_keep_tiles))
                def _(kv_ref=kv_ref, c0=c0, keep_rows=keep_rows):
                    r0 = acc.shape[0] - keep_rows
                    flat = kv_ref.reshape(keep_rows * h_g, LANES)
                    for h in range(h_g):
                        flat[pl.ds(h, keep_rows, stride=h_g), :] = (
                            acc[r0:, (c0 + h) * LANES:(c0 + h + 1) * LANES])

    @pl.when(i >= n_p)
    def _():
        cast_arrived_chunk()
        acc = jnp.dot(hs_ref[...], wbf_ref[cur], preferred_element_type=F32)
        us_ref[...] = acc

        @pl.when((j >= j_lo) & (j < j_hi))
        def _():
            rows, parts = uq_ref.shape[0], uq_ref.shape[1]
            flat = uq_ref.reshape(rows * parts * h_g, LANES)
            for c in range(nc):
                flat[pl.ds(c, rows, stride=parts * h_g), :] = acc[:, c * LANES:(c + 1) * LANES]


def _inproj(h_p, h_s, w_f32, *, batch, seq, h_g, attn_off, attn_parts):
    mp, d = h_p.shape
    ms = h_s.shape[0]
    n = w_f32.shape[1]
    tm = ROW_TILE
    tn = _largest_divisor(n, (1024, 512, 256, 128))
    n_p, n_s = mp // tm, ms // tm
    n_chunks = n_p + n_s - 1
    assert d % n_chunks == 0 and (d // n_chunks) % 16 == 0
    kc = d // n_chunks
    nc = tn // LANES
    assert nc % h_g == 0 and (attn_off * LANES) % tn == 0 and (attn_parts * h_g) % nc == 0
    ppt = nc // h_g
    j_lo = attn_off * LANES // tn
    n_aj = attn_parts // ppt
    assert seq % tm == 0
    tiles_per_seq = seq // tm

    kv_plan, kv_shapes, kv_specs = [], [], []
    for g, (window, _) in enumerate(DIL_GROUPS):
        keep = min(window, seq)
        blk_rows = min(keep, tm)
        assert keep % blk_rows == 0 and seq % keep == 0
        nkt = keep // blk_rows
        k_part, v_part = N_GROUPS + g, 2 * N_GROUPS + g
        jk, ck = j_lo + k_part // ppt, (k_part % ppt) * h_g
        jv, cv = j_lo + v_part // ppt, (v_part % ppt) * h_g
        assert jk < jv
        kv_plan.append((jk, ck, jv, cv, nkt))
        kv_shapes.append(jax.ShapeDtypeStruct((batch * keep, 2, h_g, LANES), F32))
        last = batch * nkt - 1

        def kv_index(j, i, jk=jk, jv=jv, nkt=nkt, last=last):
            ip = jnp.minimum(i, n_p - 1)
            moving = (ip // tiles_per_seq) * nkt + jnp.clip(ip % tiles_per_seq - (tiles_per_seq - nkt), 0, nkt - 1)
            rb = jnp.where(j < jk, 0, jnp.where((j == jk) | (j == jv), moving, last))
            return (rb, (j >= jv).astype(jnp.int32), 0, 0)

        kv_specs.append(pl.BlockSpec((blk_rows, 1, h_g, LANES), kv_index))

    kern = functools.partial(_inproj_kernel, n_p=n_p, nc=nc, h_g=h_g, j_lo=j_lo, j_hi=j_lo + n_aj,
                             kv_plan=tuple(kv_plan), tiles_per_seq=tiles_per_seq, n_chunks=n_chunks)
    return pl.pallas_call(
        kern,
        out_shape=(jax.ShapeDtypeStruct((n // LANES, mp, LANES), F32),
                   jax.ShapeDtypeStruct((ms, n), F32),
                   jax.ShapeDtypeStruct((ms, attn_parts, h_g, LANES), F32),
                   *kv_shapes),
        grid=(n // tn, n_p + n_s),
        in_specs=[
            pl.BlockSpec((tm, d), lambda j, i: (jnp.minimum(i, n_p - 1), 0)),
            pl.BlockSpec((tm, d), lambda j, i: (jnp.clip(i - n_p, 0, n_s - 1), 0),
                         pipeline_mode=pl.Buffered(1) if n_s == 1 else None),
            pl.BlockSpec(memory_space=pl.ANY),
        ],
        out_specs=(
            pl.BlockSpec((nc, tm, LANES), lambda j, i: (j, jnp.minimum(i, n_p - 1), 0)),
            pl.BlockSpec((tm, tn), lambda j, i: (jnp.clip(i - n_p, 0, n_s - 1), j)),
            pl.BlockSpec((tm, ppt, h_g, LANES),
                         lambda j, i: (jnp.clip(i - n_p, 0, n_s - 1), jnp.clip(j - j_lo, 0, n_aj - 1), 0, 0)),
            *kv_specs,
        ),
        scratch_shapes=[pltpu.VMEM((2, d, tn), BF16),
                        pltpu.VMEM((2, kc, tn), F32),
                        pltpu.SemaphoreType.DMA((2,))],
        compiler_params=_cparams(("arbitrary", "arbitrary")),
        name="inproj",
    )(h_p, h_s, w_f32)


def _lower_bound(lb_logits, layer):
    mx = jnp.max(lb_logits, axis=0, keepdims=True)
    e = jnp.exp(lb_logits - mx)
    den = jnp.sum(e, axis=0, keepdims=True)
    num = e[0:1]
    for r in range(1, layer + 1):
        num = num + e[r:r + 1]
    return num / den


def _split3_bf16(x):
    hi = x.astype(BF16)
    r1 = x - hi.astype(F32)
    mid = r1.astype(BF16)
    lo = (r1 - mid.astype(F32)).astype(BF16)
    return hi, mid, lo


def _hgrn_prompt_kernel(q_ref, f_ref, i_ref, z_ref, lb_ref, nw_ref, y_ref, sfin_ref, st_ref,
                        *, hb, tb, chunk, cpb, layer):
    t = pl.program_id(2)

    @pl.when(t == 0)
    def _():
        st_ref[...] = jnp.zeros_like(st_ref)

    lb_all = _lower_bound(lb_ref[...], layer)
    nw = nw_ref[...]
    rb = cpb * chunk
    row = lax.broadcasted_iota(jnp.int32, (rb, rb), 0)
    col = lax.broadcasted_iota(jnp.int32, (rb, rb), 1)
    causal = (row >= col) & (row // chunk == col // chunk)
    tril = jnp.where(causal, 1.0, 0.0).astype(BF16)
    dk = q_ref.shape[-1]
    qscale = float(dk) ** -0.5
    contract_last = (((1,), (1,)), ((), ()))
    contract_first = (((0,), (0,)), ((), ()))

    def block_body(blk, carry):
        r0 = pl.multiple_of(blk * rb, rb)
        for hl in range(hb):
            lb = lb_all[:, hl * LANES:(hl + 1) * LANES]
            qa = q_ref[hl, pl.ds(r0, rb), :]
            fa = f_ref[hl, pl.ds(r0, rb), :]
            v = i_ref[hl, pl.ds(r0, rb), :]
            za = z_ref[hl, pl.ds(r0, rb), :]
            q = _silu(qa) * qscale
            f = lb + (1.0 - lb) * _sigmoid(fa)
            log_f = jnp.log(f)
            k = 1.0 - f
            hi, mid, lo = _split3_bf16(log_f)
            parts = jnp.dot(tril, jnp.concatenate([hi, mid, lo], axis=1),
                            preferred_element_type=F32)
            g = (parts[:, 0:LANES] + parts[:, LANES:2 * LANES]) + parts[:, 2 * LANES:3 * LANES]
            g_last = [g[(c + 1) * chunk - 1:(c + 1) * chunk, :] for c in range(cpb)]
            g_last_rows = jnp.concatenate(
                [jnp.broadcast_to(gl, (chunk, LANES)) for gl in g_last], axis=0)
            q_dec = (q * jnp.exp(g)).astype(BF16)
            k_rel = (k * jnp.exp(-g)).astype(BF16)
            k_end = (k * jnp.exp(g_last_rows - g)).astype(BF16)
            vb = v.astype(BF16)
            a = lax.dot_general(q_dec, k_rel, contract_last, preferred_element_type=F32)
            a = jnp.where(causal, a, 0.0)
            o_intra = jnp.dot(a.astype(BF16), vb, preferred_element_type=F32)
            st = st_ref[hl]
            o_inter = []
            for c in range(cpb):
                rows = slice(c * chunk, (c + 1) * chunk)
                o_inter.append(lax.dot_general(q_dec[rows], st.astype(BF16), contract_last,
                                               preferred_element_type=F32))
                ds_t = lax.dot_general(vb[rows], k_end[rows], contract_first,
                                       preferred_element_type=F32)
                st = jnp.exp(g_last[c]) * st + ds_t
            st_ref[hl] = st
            o = o_intra + jnp.concatenate(o_inter, axis=0)
            ms = jnp.mean(o * o, axis=-1, keepdims=True)
            y = (o * lax.rsqrt(ms + RMS_EPS) * nw) * _silu(za)
            y_ref[pl.ds(r0, rb), hl * LANES:(hl + 1) * LANES] = y.astype(y_ref.dtype)
        return carry

    lax.fori_loop(0, tb // rb, block_body, 0)

    @pl.when(t == pl.num_programs(2) - 1)
    def _():
        for hl in range(hb):
            sfin_ref[0, hl] = st_ref[hl].T


def _hgrn_prompt(u3, lb_logits, nw, *, batch, seq, n_heads, layer):
    hb = _largest_divisor(n_heads, (8, 4, 2, 1))
    tb = _largest_divisor(seq, (1024, 512, 256, 128, 64))
    chunk = HGRN_CHUNK
    assert tb % chunk == 0
    nt = seq // tb
    nhb = n_heads // hb
    cpb = _largest_divisor(tb // chunk, (4, 2, 1))
    kern = functools.partial(_hgrn_prompt_kernel, hb=hb, tb=tb, chunk=chunk, cpb=cpb, layer=layer)

    def in_spec(part):
        return pl.BlockSpec((hb, tb, LANES), lambda b, h, t: (part * nhb + h, b * nt + t, 0))

    return pl.pallas_call(
        kern,
        out_shape=(jax.ShapeDtypeStruct((batch * seq, n_heads * LANES), BF16),
                   jax.ShapeDtypeStruct((batch, n_heads, LANES, LANES), F32)),
        grid=(batch, nhb, nt),
        in_specs=[in_spec(0), in_spec(1), in_spec(2), in_spec(3),
                  pl.BlockSpec((lb_logits.shape[0], hb * LANES), lambda b, h, t: (0, h)),
                  pl.BlockSpec((1, LANES), lambda b, h, t: (0, 0))],
        out_specs=(pl.BlockSpec((tb, hb * LANES), lambda b, h, t: (b * nt + t, h)),
                   pl.BlockSpec((1, hb, LANES, LANES), lambda b, h, t: (b, h, 0, 0))),
        scratch_shapes=[pltpu.VMEM((hb, LANES, LANES), F32)],
        compiler_params=_cparams(("parallel", "parallel", "arbitrary")),
        name="hgrn_prompt",
    )(u3, u3, u3, u3, lb_logits, nw)


def _hgrn_sample_kernel(q_ref, f_ref, i_ref, z_ref, lb_ref, nw_ref, s_ref, y_ref, so_ref, o_scr,
                        *, bb, td, layer):
    lb = _lower_bound(lb_ref[...], layer)
    dk = q_ref.shape[-1]
    q = _silu(q_ref[...]) * (float(dk) ** -0.5)
    f = lb + (1.0 - lb) * _sigmoid(f_ref[...])
    k = 1.0 - f
    v = i_ref[...]
    rows = bb * td
    sub = 8
    contract_last = (((1,), (1,)), ((), ()))
    contract_first = (((0,), (0,)), ((), ()))
    t_idx = lax.broadcasted_iota(jnp.int32, (rows, LANES), 0) % td
    p_inc = f
    s = 1
    while s < td:
        p_inc = p_inc * jnp.where(t_idx >= s, pltpu.roll(p_inc, s, axis=0), 1.0)
        s *= 2
    e_suf = jnp.where(t_idx + 1 < td, pltpu.roll(f, rows - 1, axis=0), 1.0)
    s = 1
    while s < td:
        e_suf = e_suf * jnp.where(t_idx + s < td, pltpu.roll(e_suf, rows - s, axis=0), 1.0)
        s *= 2
    q_dec = q * p_inc
    k_end = k * e_suf
    a = lax.dot_general(q_dec.astype(BF16), (k / p_inc).astype(BF16), contract_last,
                        preferred_element_type=F32)
    ri = lax.broadcasted_iota(jnp.int32, (rows, rows), 0)
    ci = lax.broadcasted_iota(jnp.int32, (rows, rows), 1)
    a = jnp.where((ri // td == ci // td) & (ci <= ri), a, 0.0)
    o_intra = jnp.dot(a.astype(BF16), v.astype(BF16), preferred_element_type=F32)
    dec_t = p_inc.T
    row8 = lax.broadcasted_iota(jnp.int32, (sub, LANES), 0)
    for grp in range(rows // sub):
        sl = slice(grp * sub, (grp + 1) * sub)
        qd8 = q_dec[sl].astype(BF16)
        o8 = jnp.zeros((sub, LANES), F32)
        for u in range(sub // td):
            bl = grp * (sub // td) + u
            mine = (row8 >= u * td) & (row8 < (u + 1) * td)
            s0 = s_ref[bl, 0]
            o8 = jnp.where(mine, jnp.dot(qd8, s0.astype(BF16), preferred_element_type=F32), o8)
            ds = lax.dot_general(jnp.where(mine, k_end[sl], 0.0).astype(BF16),
                                 jnp.where(mine, v[sl], 0.0).astype(BF16), contract_first,
                                 preferred_element_type=F32)
            j = bl * td + td - 1
            so_ref[bl, 0] = dec_t[:, j:j + 1] * s0 + ds
        o_scr[sl, :] = o8
    o = o_intra + o_scr[...]
    ms = jnp.mean(o * o, axis=-1, keepdims=True)
    y = (o * lax.rsqrt(ms + RMS_EPS) * nw_ref[...]) * _silu(z_ref[...])
    y_ref[...] = y.astype(y_ref.dtype)


def _hgrn_sample(us, state, lb_logits, nw, *, dec_batch, td, n_heads, layer):
    rows = LANES
    assert rows % td == 0 and 8 % td == 0
    bb = rows // td
    assert dec_batch % bb == 0
    kern = functools.partial(_hgrn_sample_kernel, bb=bb, td=td, layer=layer)

    def in_spec(part):
        return pl.BlockSpec((rows, LANES), lambda bi, h: (bi, part * n_heads + h))

    return pl.pallas_call(
        kern,
        out_shape=(jax.ShapeDtypeStruct((dec_batch * td, n_heads * LANES), BF16),
                   jax.ShapeDtypeStruct(state.shape, F32)),
        grid=(dec_batch // bb, n_heads),
        in_specs=[in_spec(0), in_spec(1), in_spec(2), in_spec(3),
                  pl.BlockSpec((lb_logits.shape[0], LANES), lambda bi, h: (0, h)),
                  pl.BlockSpec((1, LANES), lambda bi, h: (0, 0)),
                  pl.BlockSpec((bb, 1, LANES, LANES), lambda bi, h: (bi, h, 0, 0))],
        out_specs=(pl.BlockSpec((rows, LANES), lambda bi, h: (bi, h)),
                   pl.BlockSpec((bb, 1, LANES, LANES), lambda bi, h: (bi, h, 0, 0))),
        scratch_shapes=[pltpu.VMEM((rows, LANES), F32)],
        compiler_params=_cparams(("parallel", "parallel")),
        name="hgrn_sample",
    )(us, us, us, us, lb_logits, nw, state)


def _attn_prompt_kernel(q_ref, k_ref, v_ref, z_ref, y_ref, m_acc, den_acc, o_acc, *, seq, scale):
    g = pl.program_id(2)
    neg_inf = -jnp.inf
    contract_last = (((1,), (1,)), ((), ()))

    def run_group(first, window, dil):
        wk = window // dil
        nb = seq // window
        ii = lax.broadcasted_iota(jnp.int32, (wk, wk), 0)
        jj = lax.broadcasted_iota(jnp.int32, (wk, wk), 1)
        mask_prev = jj >= ii
        mask_own = jj <= ii
        ones = jnp.ones((wk, LANES), BF16)

        def rows(start):
            return pl.ds(start, wk, stride=dil) if dil > 1 else pl.ds(start, wk)

        def block(idx, carry):
            n, r = idx // dil, idx % dil
            own = rows(n * window + r)
            prev = rows(jnp.maximum(n - 1, 0) * window + r)
            q = q_ref[0, own, :].astype(BF16)
            k_own = k_ref[0, own, :].astype(BF16)
            k_prev = k_ref[0, prev, :].astype(BF16)
            v_own = jnp.concatenate([v_ref[0, own, :].astype(BF16), ones], axis=1)
            v_prev = jnp.concatenate([v_ref[0, prev, :].astype(BF16), ones], axis=1)
            s_own = lax.dot_general(q, k_own, contract_last, preferred_element_type=F32) * scale
            s_prev = lax.dot_general(q, k_prev, contract_last, preferred_element_type=F32) * scale
            no_prev = jnp.where(n > 0, 0.0, neg_inf)
            s_own = jnp.where(mask_own, s_own, neg_inf)
            s_prev = jnp.where(mask_prev, s_prev + no_prev, neg_inf)
            m = jnp.max(jnp.maximum(s_own, s_prev), axis=-1, keepdims=True)
            p_own = jnp.exp(s_own - m).astype(BF16)
            p_prev = jnp.exp(s_prev - m).astype(BF16)
            acc = (jnp.dot(p_own, v_own, preferred_element_type=F32)
                   + jnp.dot(p_prev, v_prev, preferred_element_type=F32))
            o, den = acc[:, :LANES], acc[:, LANES:]
            if first:
                m_acc[own, :] = jnp.broadcast_to(m, (wk, LANES))
                den_acc[own, :] = den
                o_acc[own, :] = o
            else:
                m_old = m_acc[own, :]
                m_new = jnp.maximum(m_old, m)
                a_old = jnp.exp(m_old - m_new)
                a_new = jnp.exp(m - m_new)
                m_acc[own, :] = m_new
                den_acc[own, :] = den_acc[own, :] * a_old + den * a_new
                o_acc[own, :] = o_acc[own, :] * a_old + o * a_new
            return carry

        lax.fori_loop(0, nb * dil, block, 0, unroll=8)

    for gi, (window, dil) in enumerate(DIL_GROUPS):
        @pl.when(g == N_GROUPS - 1 - gi)
        def _(gi=gi, window=window, dil=dil):
            run_group(gi == N_GROUPS - 1, window, dil)

    @pl.when(g == N_GROUPS - 1)
    def _():
        y = (o_acc[...] / den_acc[...]) * _silu(z_ref[0])
        y_ref[...] = y.astype(y_ref.dtype)


def _attn_prompt(u3, *, batch, seq, h_g, q_off, k_off, v_off, z_off):
    for window, _ in DIL_GROUPS:
        assert seq % window == 0
    kern = functools.partial(_attn_prompt_kernel, seq=seq, scale=float(LANES) ** -0.5)

    def in_spec(off):
        return pl.BlockSpec((1, seq, LANES),
                            lambda b, h, g: (off + (N_GROUPS - 1 - g) * h_g + h, b, 0))

    return pl.pallas_call(
        kern,
        out_shape=jax.ShapeDtypeStruct((batch * seq, h_g * LANES), BF16),
        grid=(batch, h_g, N_GROUPS),
        in_specs=[in_spec(q_off), in_spec(k_off), in_spec(v_off),
                  pl.BlockSpec((1, seq, LANES), lambda b, h, g: (z_off + h, b, 0))],
        out_specs=pl.BlockSpec((seq, LANES), lambda b, h, g: (b, h)),
        scratch_shapes=[pltpu.VMEM((seq, LANES), F32)] * 3,
        compiler_params=_cparams(("parallel", "parallel", "arbitrary")),
        name="attn_prompt",
    )(u3, u3, u3, u3)


def _attn_sample_kernel(uq_ref, c0_ref, c1_ref, c2_ref, y_ref, *, td, h_g):
    c_refs = (c0_ref, c1_ref, c2_ref)
    b = pl.program_id(0)
    rows_blk = uq_ref.shape[0]
    per_blk = rows_blk // td
    ro = (b % per_blk) * td
    qscale = float(LANES) ** -0.5 * 1.4426950408889634
    neg_inf = -jnp.inf
    contract_last = (((1,), (1,)), ((), ()))
    q_part, k_part, v_part, z_part = 0, N_GROUPS, 2 * N_GROUPS, 3 * N_GROUPS
    wk = DIL_GROUPS[0][0] // DIL_GROUPS[0][1]
    nl = N_GROUPS * h_g
    shape3 = (wk, h_g, nl)
    c_row = lax.broadcasted_iota(jnp.int32, shape3, 0)
    key_head = lax.broadcasted_iota(jnp.int32, shape3, 1)
    lane = lax.broadcasted_iota(jnp.int32, shape3, 2)
    head_match = lane % h_g == key_head
    e_row = lax.broadcasted_iota(jnp.int32, (nl, LANES), 0)
    expand = [jnp.where(e_row // h_g == g, 1.0, 0.0).astype(BF16) for g in range(N_GROUPS)]
    zeros_q = jnp.zeros((h_g, LANES), F32)

    for i in range(td):
        r = ro + i
        valid = head_match
        k_parts, q_rows, v_cs, s_new, v_new = [], [], [], [], []
        for g, (window, dil) in enumerate(DIL_GROUPS):
            assert window // dil == wk
            c_ref = c_refs[g]
            q_t = uq_ref[r, q_part + g]
            if dil == 1:
                k_c, v_c = c_ref[0, :, 0], c_ref[0, :, 1]
                new_js = tuple(range(i + 1))
                valid = valid & ((lane // h_g != g) | (c_row >= i))
            else:
                k_c, v_c = c_ref[0, :, i, 0], c_ref[0, :, i, 1]
                new_js = (i,)
            k_parts.append(k_c.reshape(wk * h_g, LANES).astype(BF16))
            q_rows.append(jnp.concatenate(
                [q_t * qscale if gg == g else zeros_q for gg in range(N_GROUPS)], axis=1))
            v_cs.append(v_c)
            for jn in new_js:
                s_new.append(jnp.sum(uq_ref[ro + jn, k_part + g] * q_t, axis=-1, keepdims=True) * qscale)
                v_new.append(uq_ref[ro + jn, v_part + g])
        s = lax.dot_general(jnp.concatenate(k_parts, axis=1),
                            jnp.concatenate(q_rows, axis=0).astype(BF16),
                            contract_last, preferred_element_type=F32)
        s = jnp.where(valid, s.reshape(shape3), neg_inf)
        m = jnp.max(jnp.max(s, axis=0), axis=-1, keepdims=True)
        for sn in s_new:
            m = jnp.maximum(m, sn)
        p = jnp.exp2(s - m[None])
        den = jnp.sum(jnp.sum(p, axis=0), axis=-1, keepdims=True)
        p_bf = p.astype(BF16).reshape(wk * h_g, nl)
        y = jnp.zeros((h_g, LANES), F32)
        for g in range(N_GROUPS):
            pb = jnp.dot(p_bf, expand[g], preferred_element_type=F32)
            y = y + jnp.sum(pb.reshape(wk, h_g, LANES) * v_cs[g], axis=0)
        for sn, vn in zip(s_new, v_new):
            pn = jnp.exp2(sn - m)
            den = den + pn
            y = y + pn * vn
        y_ref[r] = (y / den) * _silu(uq_ref[r, z_part])


def _attn_sample(uq, caches, *, dec_batch, td, h_g):
    rows_blk = 8
    assert rows_blk % td == 0 and (dec_batch * td) % rows_blk == 0
    per_blk = rows_blk // td
    parts = uq.shape[1]
    views, c_specs = [], []
    for (window, dil), c in zip(DIL_GROUPS, caches):
        assert c.shape[1] == window, "window buffers must hold a full window"
        wk = window // dil
        if dil == 1:
            views.append(c)
            c_specs.append(pl.BlockSpec((1, wk, 2, h_g, LANES), lambda b: (b, 0, 0, 0, 0)))
        else:
            assert td <= dil
            views.append(c.reshape(dec_batch, wk, dil, 2, h_g, LANES))
            c_specs.append(pl.BlockSpec((1, wk, td, 2, h_g, LANES), lambda b: (b, 0, 0, 0, 0, 0)))
    kern = functools.partial(_attn_sample_kernel, td=td, h_g=h_g)
    return pl.pallas_call(
        kern,
        out_shape=jax.ShapeDtypeStruct((dec_batch * td, h_g, LANES), F32),
        grid=(dec_batch,),
        in_specs=[pl.BlockSpec((rows_blk, parts, h_g, LANES), lambda b: (b // per_blk, 0, 0, 0))] + c_specs,
        out_specs=pl.BlockSpec((rows_blk, h_g, LANES), lambda b: (b // per_blk, 0, 0)),
        compiler_params=_cparams(("arbitrary",)),
        name="attn_sample",
    )(uq, *views)


def _merge_kernel(ya_ref, yb_ref, wa_ref, wb_ref, ga_ref, gb_ref, o_ref, *, nc, blocked_gates):
    pa = jnp.dot(ya_ref[...].astype(BF16), wa_ref[...], preferred_element_type=F32)
    pb = jnp.dot(yb_ref[...].astype(BF16), wb_ref[...], preferred_element_type=F32)
    for c in range(nc):
        ls = slice(c * LANES, (c + 1) * LANES)
        ga = ga_ref[c] if blocked_gates else ga_ref[:, ls]
        gb = gb_ref[c] if blocked_gates else gb_ref[:, ls]
        o_ref[:, ls] = (_sigmoid(ga) * pa[:, ls] + _sigmoid(gb) * pb[:, ls]).astype(o_ref.dtype)


def _merge(ya, yb, wa_bf, wb_bf, gates, *, ga_off, gb_off, blocked_gates):
    m = ya.shape[0]
    d = wa_bf.shape[1]
    tm = ROW_TILE
    tn = _largest_divisor(d, (1024, 512, 256, 128))
    nc = tn // LANES
    assert (ga_off * LANES) % tn == 0 and (gb_off * LANES) % tn == 0
    ga_blk, gb_blk = ga_off * LANES // tn, gb_off * LANES // tn
    if blocked_gates:
        ga_spec = pl.BlockSpec((nc, tm, LANES), lambda j, i: (ga_blk + j, i, 0))
        gb_spec = pl.BlockSpec((nc, tm, LANES), lambda j, i: (gb_blk + j, i, 0))
    else:
        ga_spec = pl.BlockSpec((tm, tn), lambda j, i: (i, ga_blk + j))
        gb_spec = pl.BlockSpec((tm, tn), lambda j, i: (i, gb_blk + j))
    kern = functools.partial(_merge_kernel, nc=nc, blocked_gates=blocked_gates)
    return pl.pallas_call(
        kern,
        out_shape=jax.ShapeDtypeStruct((m, d), BF16),
        grid=(d // tn, m // tm),
        in_specs=[pl.BlockSpec((tm, ya.shape[1]), lambda j, i: (i, 0)),
                  pl.BlockSpec((tm, yb.shape[1]), lambda j, i: (i, 0)),
                  pl.BlockSpec((wa_bf.shape[0], tn), lambda j, i: (0, j)),
                  pl.BlockSpec((wb_bf.shape[0], tn), lambda j, i: (0, j)),
                  ga_spec, gb_spec],
        out_specs=pl.BlockSpec((tm, tn), lambda j, i: (i, j)),
        compiler_params=_cparams(("parallel", "parallel")),
        name="merge_blocked" if blocked_gates else "merge_rows",
    )(ya, yb, wa_bf, wb_bf, gates, gates)


def _outproj_kernel(mg_ref, w_ref, x_ref, fw_ref, y_ref, *, nj, tn):
    j = pl.program_id(1)
    for jj in range(nj):
        @pl.when(j == jj)
        def _(jj=jj):
            y_ref[:, jj * tn:(jj + 1) * tn] = x_ref[...] + jnp.dot(
                mg_ref[...], w_ref[...], preferred_element_type=F32)

    @pl.when(j == nj - 1)
    def _():
        rc = 64
        fw = fw_ref[...]

        def norm_rows(c, carry):
            rows = pl.ds(pl.multiple_of(c * rc, rc), rc)
            full = y_ref[rows, :]
            ms = jnp.mean(full * full, axis=-1, keepdims=True)
            y_ref[rows, :] = full * lax.rsqrt(ms + RMS_EPS) * fw
            return carry

        lax.fori_loop(0, y_ref.shape[0] // rc, norm_rows, 0)


def _outproj(merged, w_bf, x, fw):
    m, d = x.shape
    tm = ROW_TILE
    tn = _largest_divisor(d, (1024, 512, 256, 128))
    nj = d // tn
    kern = functools.partial(_outproj_kernel, nj=nj, tn=tn)
    return pl.pallas_call(
        kern,
        out_shape=jax.ShapeDtypeStruct((m, d), F32),
        grid=(m // tm, nj),
        in_specs=[pl.BlockSpec((tm, d), lambda i, j: (i, 0)),
                  pl.BlockSpec((d, tn), lambda i, j: (0, j)),
                  pl.BlockSpec((tm, tn), lambda i, j: (i, j)),
                  pl.BlockSpec((1, d), lambda i, j: (0, 0))],
        out_specs=pl.BlockSpec((tm, d), lambda i, j: (i, 0)),
        compiler_params=_cparams(("parallel", "arbitrary")),
        name="outproj_norm",
    )(merged, w_bf, x, fw.reshape(1, d))


def kernel(x_prompt, x_sample, cache_kv_w128, cache_kv_w512, cache_kv_w2048, state_hgrn,
           norm_w, w_in, lb_logits, hgrn_norm_w, w_proj_a, w_proj_b, w_out, final_norm_w):
    depth = norm_w.shape[0]
    assert depth == 1, "single-layer step"
    layer = 0
    batch, seq, d = x_prompt.shape
    dec_batch, td, _ = x_sample.shape
    w_a = w_proj_a.shape[1]
    w_bo = w_proj_b.shape[1]
    n_heads = w_a // LANES
    h_g = w_bo // LANES
    n_b = N_GROUPS * h_g
    n_d = d // LANES
    qb_off = 4 * n_heads
    kb_off, vb_off = qb_off + n_b, qb_off + 2 * n_b
    zb_off = qb_off + 3 * n_b
    ga_off = zb_off + h_g
    gb_off = ga_off + n_d
    assert (gb_off + n_d) * LANES == w_in.shape[2]

    xp = x_prompt.reshape(batch * seq, d)
    xs = x_sample.reshape(dec_batch * td, d)
    caches = (cache_kv_w128[layer], cache_kv_w512[layer], cache_kv_w2048[layer])

    wa_bf = _cast_bf16(w_proj_a[layer])
    wb_bf = _cast_bf16(w_proj_b[layer])
    wo_bf = _cast_bf16(w_out[layer])

    h_p = _norm_cast(xp, norm_w[layer])
    h_s = _norm_cast(xs, norm_w[layer])
    u3, us, uq, *kv_p = _inproj(h_p, h_s, w_in[layer], batch=batch, seq=seq, h_g=h_g,
                                attn_off=qb_off, attn_parts=3 * N_GROUPS + 1)
    kv_p = [kv.reshape(1, batch, kv.shape[0] // batch, 2, h_g, LANES) for kv in kv_p]

    nw = hgrn_norm_w[layer].reshape(1, LANES)
    ya_p, s_p = _hgrn_prompt(u3, lb_logits, nw, batch=batch, seq=seq, n_heads=n_heads, layer=layer)
    ya_s, s_s = _hgrn_sample(us, state_hgrn[layer], lb_logits, nw,
                             dec_batch=dec_batch, td=td, n_heads=n_heads, layer=layer)

    yb_p = _attn_prompt(u3, batch=batch, seq=seq, h_g=h_g,
                        q_off=qb_off, k_off=kb_off, v_off=vb_off, z_off=zb_off)
    yb_s = _attn_sample(uq, caches, dec_batch=dec_batch, td=td, h_g=h_g)
    yb_s = yb_s.reshape(dec_batch * td, h_g * LANES)

    mg_p = _merge(ya_p, yb_p, wa_bf, wb_bf, u3, ga_off=ga_off, gb_off=gb_off, blocked_gates=True)
    mg_s = _merge(ya_s, yb_s, wa_bf, wb_bf, us, ga_off=ga_off, gb_off=gb_off, blocked_gates=False)
    y_p = _outproj(mg_p, wo_bf, xp, final_norm_w)
    y_s = _outproj(mg_s, wo_bf, xs, final_norm_w)

    kv_s = [jnp.stack([uq[:, N_GROUPS + g], uq[:, 2 * N_GROUPS + g]], axis=1)
            .reshape(1, dec_batch, td, 2, h_g, LANES) for g in range(N_GROUPS)]

    return (y_p.reshape(batch, seq, d), y_s.reshape(dec_batch, td, d),
            kv_p[0], kv_p[1], kv_p[2], s_p[None],
            kv_s[0], kv_s[1], kv_s[2], s_s[None])
```

```python
import functools

import jax
import jax.numpy as jnp
from jax import lax
from jax.experimental import pallas as pl
from jax.experimental.pallas import tpu as pltpu

F32 = jnp.float32
BF16 = jnp.bfloat16

LANES = 128
RMS_EPS = 1e-6
HGRN_CHUNK = 64
DIL_GROUPS = ((128, 1), (512, 4), (2048, 16))
N_GROUPS = len(DIL_GROUPS)
ROW_TILE = 512
VMEM_LIMIT = 56 * 1024 * 1024


def _cparams(semantics, vmem=VMEM_LIMIT):
    return pltpu.CompilerParams(dimension_semantics=semantics, vmem_limit_bytes=vmem)


def _sigmoid(x):
    return 1.0 / (1.0 + jnp.exp(-x))


def _silu(x):
    return x * _sigmoid(x)


def _largest_divisor(n, candidates):
    for c in candidates:
        if n % c == 0:
            return c
    raise ValueError(f"no tile in {candidates} divides {n}")


def _norm_cast_kernel(x_ref, w_ref, o_ref):
    x = x_ref[...]
    ms = jnp.mean(x * x, axis=-1, keepdims=True)
    o_ref[...] = (x * lax.rsqrt(ms + RMS_EPS) * w_ref[...]).astype(o_ref.dtype)


def _norm_cast(x, w):
    m, d = x.shape
    tm = _largest_divisor(m, (ROW_TILE, 256, 128))
    return pl.pallas_call(
        _norm_cast_kernel,
        out_shape=jax.ShapeDtypeStruct((m, d), BF16),
        grid=(m // tm,),
        in_specs=[pl.BlockSpec((tm, d), lambda i: (i, 0)),
                  pl.BlockSpec((1, d), lambda i: (0, 0))],
        out_specs=pl.BlockSpec((tm, d), lambda i: (i, 0)),
        compiler_params=_cparams(("parallel",)),
        name="norm_cast",
    )(x, w.reshape(1, d))


def _cast_kernel(x_ref, o_ref):
    o_ref[...] = x_ref[...].astype(o_ref.dtype)


def _cast_bf16(w):
    r, c = w.shape
    tr = _largest_divisor(r, (512, 256, 128))
    tc = _largest_divisor(c, (2048, 1024, 512, 256, 128))
    return pl.pallas_call(
        _cast_kernel,
        out_shape=jax.ShapeDtypeStruct((r, c), BF16),
        grid=(r // tr, c // tc),
        in_specs=[pl.BlockSpec((tr, tc), lambda i, j: (i, j))],
        out_specs=pl.BlockSpec((tr, tc), lambda i, j: (i, j)),
        compiler_params=_cparams(("parallel", "parallel")),
        name="cast_bf16",
    )(w)


def _inproj_kernel(hp_ref, hs_ref, w_hbm, u3_ref, us_ref, uq_ref, *rest,
                   n_p, nc, h_g, j_lo, j_hi, kv_plan, tiles_per_seq, n_chunks):
    kv_refs = rest[:len(kv_plan)]
    wbf_ref, stage_ref, sem = rest[len(kv_plan):]
    j = pl.program_id(0)
    i = pl.program_id(1)
    _, kc, tn = stage_ref.shape
    cur = j % 2
    nxt = 1 - cur

    def chunk_copy(tile, c, slot):
        return pltpu.make_async_copy(
            w_hbm.at[pl.ds(pl.multiple_of(c * kc, kc), kc), pl.ds(pl.multiple_of(tile * tn, tn), tn)],
            stage_ref.at[slot], sem.at[slot])

    def cast_chunk(c, slot, buf):
        wbf_ref[buf, pl.ds(pl.multiple_of(c * kc, kc), kc), :] = stage_ref[slot].astype(BF16)

    @pl.when((j == 0) & (i == 0))
    def _():
        def first_tile(c, carry):
            cp = chunk_copy(0, c, c % 2)
            cp.start()
            cp.wait()
            cast_chunk(c, c % 2, 0)
            return carry
        lax.fori_loop(0, n_chunks, first_tile, 0)

    has_next = j + 1 < pl.num_programs(0)

    @pl.when(has_next & (i >= 1) & (i <= n_chunks))
    def _():
        chunk_copy(j + 1, i - 1, (i + 1) % 2).wait()

    @pl.when(has_next & (i < n_chunks))
    def _():
        chunk_copy(j + 1, i, i % 2).start()

    def cast_arrived_chunk():
        cast_chunk(jnp.clip(i - 1, 0, n_chunks - 1), (i + 1) % 2, nxt)

    @pl.when(i < n_p)
    def _():
        cast_arrived_chunk()
        acc = jnp.dot(hp_ref[...], wbf_ref[cur], preferred_element_type=F32)
        for c in range(nc):
            u3_ref[c] = acc[:, c * LANES:(c + 1) * LANES]

        tile_in_seq = i % tiles_per_seq
        for kv_ref, (jk, ck, jv, cv, n_keep_tiles) in zip(kv_refs, kv_plan):
            keep_rows = kv_ref.shape[0]
            for jw, c0 in ((jk, ck), (jv, cv)):
                @pl.when((j == jw) & (tile_in_seq >= tiles_per_seq - n_keep_tiles))
                def _(kv_ref=kv_ref, c0=c0, keep_rows=keep_rows):
                    r0 = acc.shape[0] - keep_rows
                    flat = kv_ref.reshape(keep_rows * h_g, LANES)
                    for h in range(h_g):
                        flat[pl.ds(h, keep_rows, stride=h_g), :] = (
                            acc[r0:, (c0 + h) * LANES:(c0 + h + 1) * LANES])

    @pl.when(i >= n_p)
    def _():
        cast_arrived_chunk()
        acc = jnp.dot(hs_ref[...], wbf_ref[cur], preferred_element_type=F32)
        us_ref[...] = acc

        @pl.when((j >= j_lo) & (j < j_hi))
        def _():
            rows, parts = uq_ref.shape[0], uq_ref.shape[1]
            flat = uq_ref.reshape(rows * parts * h_g, LANES)
            for c in range(nc):
                flat[pl.ds(c, rows, stride=parts * h_g), :] = acc[:, c * LANES:(c + 1) * LANES]


def _inproj(h_p, h_s, w_f32, *, batch, seq, h_g, attn_off, attn_parts):
    mp, d = h_p.shape
    ms = h_s.shape[0]
    n = w_f32.shape[1]
    tm = ROW_TILE
    tn = _largest_divisor(n, (1024, 512, 256, 128))
    n_p, n_s = mp // tm, ms // tm
    n_chunks = n_p + n_s - 1
    assert d % n_chunks == 0 and (d // n_chunks) % 16 == 0
    kc = d // n_chunks
    nc = tn // LANES
    assert nc % h_g == 0 and (attn_off * LANES) % tn == 0 and (attn_parts * h_g) % nc == 0
    ppt = nc // h_g
    j_lo = attn_off * LANES // tn
    n_aj = attn_parts // ppt
    assert seq % tm == 0
    tiles_per_seq = seq // tm

    kv_plan, kv_shapes, kv_specs = [], [], []
    for g, (window, _) in enumerate(DIL_GROUPS):
        keep = min(window, seq)
        blk_rows = min(keep, tm)
        assert keep % blk_rows == 0 and seq % keep == 0
        nkt = keep // blk_rows
        k_part, v_part = N_GROUPS + g, 2 * N_GROUPS + g
        jk, ck = j_lo + k_part // ppt, (k_part % ppt) * h_g
        jv, cv = j_lo + v_part // ppt, (v_part % ppt) * h_g
        assert jk < jv
        kv_plan.append((jk, ck, jv, cv, nkt))
        kv_shapes.append(jax.ShapeDtypeStruct((batch * keep, 2, h_g, LANES), F32))
        last = batch * nkt - 1

        def kv_index(j, i, jk=jk, jv=jv, nkt=nkt, last=last):
            ip = jnp.minimum(i, n_p - 1)
            moving = (ip // tiles_per_seq) * nkt + jnp.clip(ip % tiles_per_seq - (tiles_per_seq - nkt), 0, nkt - 1)
            rb = jnp.where(j < jk, 0, jnp.where((j == jk) | (j == jv), moving, last))
            return (rb, (j >= jv).astype(jnp.int32), 0, 0)

        kv_specs.append(pl.BlockSpec((blk_rows, 1, h_g, LANES), kv_index))

    kern = functools.partial(_inproj_kernel, n_p=n_p, nc=nc, h_g=h_g, j_lo=j_lo, j_hi=j_lo + n_aj,
                             kv_plan=tuple(kv_plan), tiles_per_seq=tiles_per_seq, n_chunks=n_chunks)
    return pl.pallas_call(
        kern,
        out_shape=(jax.ShapeDtypeStruct((n // LANES, mp, LANES), F32),
                   jax.ShapeDtypeStruct((ms, n), F32),
                   jax.ShapeDtypeStruct((ms, attn_parts, h_g, LANES), F32),
                   *kv_shapes),
        grid=(n // tn, n_p + n_s),
        in_specs=[
            pl.BlockSpec((tm, d), lambda j, i: (jnp.minimum(i, n_p - 1), 0)),
            pl.BlockSpec((tm, d), lambda j, i: (jnp.clip(i - n_p, 0, n_s - 1), 0),
                         pipeline_mode=pl.Buffered(1) if n_s == 1 else None),
            pl.BlockSpec(memory_space=pl.ANY),
        ],
        out_specs=(
            pl.BlockSpec((nc, tm, LANES), lambda j, i: (j, jnp.minimum(i, n_p - 1), 0)),
            pl.BlockSpec((tm, tn), lambda j, i: (jnp.clip(i - n_p, 0, n_s - 1), j)),
            pl.BlockSpec((tm, ppt, h_g, LANES),
                         lambda j, i: (jnp.clip(i - n_p, 0, n_s - 1), jnp.clip(j - j_lo, 0, n_aj - 1), 0, 0)),
            *kv_specs,
        ),
        scratch_shapes=[pltpu.VMEM((2, d, tn), BF16),
                        pltpu.VMEM((2, kc, tn), F32),
                        pltpu.SemaphoreType.DMA((2,))],
        compiler_params=_cparams(("arbitrary", "arbitrary")),
        name="inproj",
    )(h_p, h_s, w_f32)


def _lower_bound(lb_logits, layer):
    mx = jnp.max(lb_logits, axis=0, keepdims=True)
    e = jnp.exp(lb_logits - mx)
    den = jnp.sum(e, axis=0, keepdims=True)
    num = e[0:1]
    for r in range(1, layer + 1):
        num = num + e[r:r + 1]
    return num / den


def _split3_bf16(x):
    hi = x.astype(BF16)
    r1 = x - hi.astype(F32)
    mid = r1.astype(BF16)
    lo = (r1 - mid.astype(F32)).astype(BF16)
    return hi, mid, lo


def _hgrn_prompt_kernel(q_ref, f_ref, i_ref, z_ref, lb_ref, nw_ref, y_ref, sfin_ref, st_ref,
                        *, hb, tb, chunk, cpb, layer):
    t = pl.program_id(2)

    @pl.when(t == 0)
    def _():
        st_ref[...] = jnp.zeros_like(st_ref)

    lb_all = _lower_bound(lb_ref[...], layer)
    nw = nw_ref[...]
    rb = cpb * chunk
    row = lax.broadcasted_iota(jnp.int32, (rb, rb), 0)
    col = lax.broadcasted_iota(jnp.int32, (rb, rb), 1)
    causal = (row >= col) & (row // chunk == col // chunk)
    tril = jnp.where(causal, 1.0, 0.0).astype(BF16)
    dk = q_ref.shape[-1]
    qscale = float(dk) ** -0.5
    contract_last = (((1,), (1,)), ((), ()))
    contract_first = (((0,), (0,)), ((), ()))

    def block_body(blk, carry):
        r0 = pl.multiple_of(blk * rb, rb)
        for hl in range(hb):
            lb = lb_all[:, hl * LANES:(hl + 1) * LANES]
            qa = q_ref[hl, pl.ds(r0, rb), :]
            fa = f_ref[hl, pl.ds(r0, rb), :]
            v = i_ref[hl, pl.ds(r0, rb), :]
            za = z_ref[hl, pl.ds(r0, rb), :]
            q = _silu(qa) * qscale
            f = lb + (1.0 - lb) * _sigmoid(fa)
            log_f = jnp.log(f)
            k = 1.0 - f
            hi, mid, lo = _split3_bf16(log_f)
            parts = jnp.dot(tril, jnp.concatenate([hi, mid, lo], axis=1),
                            preferred_element_type=F32)
            g = (parts[:, 0:LANES] + parts[:, LANES:2 * LANES]) + parts[:, 2 * LANES:3 * LANES]
            g_last = [g[(c + 1) * chunk - 1:(c + 1) * chunk, :] for c in range(cpb)]
            g_last_rows = jnp.concatenate(
                [jnp.broadcast_to(gl, (chunk, LANES)) for gl in g_last], axis=0)
            q_dec = (q * jnp.exp(g)).astype(BF16)
            k_rel = (k * jnp.exp(-g)).astype(BF16)
            k_end = (k * jnp.exp(g_last_rows - g)).astype(BF16)
            vb = v.astype(BF16)
            a = lax.dot_general(q_dec, k_rel, contract_last, preferred_element_type=F32)
            a = jnp.where(causal, a, 0.0)
            o_intra = jnp.dot(a.astype(BF16), vb, preferred_element_type=F32)
            st = st_ref[hl]
            o_inter = []
            for c in range(cpb):
                rows = slice(c * chunk, (c + 1) * chunk)
                o_inter.append(lax.dot_general(q_dec[rows], st.astype(BF16), contract_last,
                                               preferred_element_type=F32))
                ds_t = lax.dot_general(vb[rows], k_end[rows], contract_first,
                                       preferred_element_type=F32)
                st = jnp.exp(g_last[c]) * st + ds_t
            st_ref[hl] = st
            o = o_intra + jnp.concatenate(o_inter, axis=0)
            ms = jnp.mean(o * o, axis=-1, keepdims=True)
            y = (o * lax.rsqrt(ms + RMS_EPS) * nw) * _silu(za)
            y_ref[pl.ds(r0, rb), hl * LANES:(hl + 1) * LANES] = y.astype(y_ref.dtype)
        return carry

    lax.fori_loop(0, tb // rb, block_body, 0)

    @pl.when(t == pl.num_programs(2) - 1)
    def _():
        for hl in range(hb):
            sfin_ref[0, hl] = st_ref[hl].T


def _hgrn_prompt(u3, lb_logits, nw, *, batch, seq, n_heads, layer):
    hb = _largest_divisor(n_heads, (8, 4, 2, 1))
    tb = _largest_divisor(seq, (1024, 512, 256, 128, 64))
    chunk = HGRN_CHUNK
    assert tb % chunk == 0
    nt = seq // tb
    nhb = n_heads // hb
    cpb = _largest_divisor(tb // chunk, (4, 2, 1))
    kern = functools.partial(_hgrn_prompt_kernel, hb=hb, tb=tb, chunk=chunk, cpb=cpb, layer=layer)

    def in_spec(part):
        return pl.BlockSpec((hb, tb, LANES), lambda b, h, t: (part * nhb + h, b * nt + t, 0))

    return pl.pallas_call(
        kern,
        out_shape=(jax.ShapeDtypeStruct((batch * seq, n_heads * LANES), BF16),
                   jax.ShapeDtypeStruct((batch, n_heads, LANES, LANES), F32)),
        grid=(batch, nhb, nt),
        in_specs=[in_spec(0), in_spec(1), in_spec(2), in_spec(3),
                  pl.BlockSpec((lb_logits.shape[0], hb * LANES), lambda b, h, t: (0, h)),
                  pl.BlockSpec((1, LANES), lambda b, h, t: (0, 0))],
        out_specs=(pl.BlockSpec((tb, hb * LANES), lambda b, h, t: (b * nt + t, h)),
                   pl.BlockSpec((1, hb, LANES, LANES), lambda b, h, t: (b, h, 0, 0))),
        scratch_shapes=[pltpu.VMEM((hb, LANES, LANES), F32)],
        compiler_params=_cparams(("parallel", "parallel", "arbitrary")),
        name="hgrn_prompt",
    )(u3, u3, u3, u3, lb_logits, nw)


def _hgrn_sample_kernel(q_ref, f_ref, i_ref, z_ref, lb_ref, nw_ref, s_ref, y_ref, so_ref, o_scr,
                        *, bb, td, layer):
    lb = _lower_bound(lb_ref[...], layer)
    dk = q_ref.shape[-1]
    q = _silu(q_ref[...]) * (float(dk) ** -0.5)
    f = lb + (1.0 - lb) * _sigmoid(f_ref[...])
    k = 1.0 - f
    v = i_ref[...]
    rows = bb * td
    sub = 8
    contract_last = (((1,), (1,)), ((), ()))
    contract_first = (((0,), (0,)), ((), ()))
    t_idx = lax.broadcasted_iota(jnp.int32, (rows, LANES), 0) % td
    p_inc = f
    s = 1
    while s < td:
        p_inc = p_inc * jnp.where(t_idx >= s, pltpu.roll(p_inc, s, axis=0), 1.0)
        s *= 2
    e_suf = jnp.where(t_idx + 1 < td, pltpu.roll(f, rows - 1, axis=0), 1.0)
    s = 1
    while s < td:
        e_suf = e_suf * jnp.where(t_idx + s < td, pltpu.roll(e_suf, rows - s, axis=0), 1.0)
        s *= 2
    q_dec = q * p_inc
    k_end = k * e_suf
    a = lax.dot_general(q_dec.astype(BF16), (k / p_inc).astype(BF16), contract_last,
                        preferred_element_type=F32)
    ri = lax.broadcasted_iota(jnp.int32, (rows, rows), 0)
    ci = lax.broadcasted_iota(jnp.int32, (rows, rows), 1)
    a = jnp.where((ri // td == ci // td) & (ci <= ri), a, 0.0)
    o_intra = jnp.dot(a.astype(BF16), v.astype(BF16), preferred_element_type=F32)
    dec_t = p_inc.T
    row8 = lax.broadcasted_iota(jnp.int32, (sub, LANES), 0)
    for grp in range(rows // sub):
        sl = slice(grp * sub, (grp + 1) * sub)
        qd8 = q_dec[sl].astype(BF16)
        o8 = jnp.zeros((sub, LANES), F32)
        for u in range(sub // td):
            bl = grp * (sub // td) + u
            mine = (row8 >= u * td) & (row8 < (u + 1) * td)
            s0 = s_ref[bl, 0]
            o8 = jnp.where(mine, jnp.dot(qd8, s0.astype(BF16), preferred_element_type=F32), o8)
            ds = lax.dot_general(jnp.where(mine, k_end[sl], 0.0).astype(BF16),
                                 jnp.where(mine, v[sl], 0.0).astype(BF16), contract_first,
                                 preferred_element_type=F32)
            j = bl * td + td - 1
            so_ref[bl, 0] = dec_t[:, j:j + 1] * s0 + ds
        o_scr[sl, :] = o8
    o = o_intra + o_scr[...]
    ms = jnp.mean(o * o, axis=-1, keepdims=True)
    y = (o * lax.rsqrt(ms + RMS_EPS) * nw_ref[...]) * _silu(z_ref[...])
    y_ref[...] = y.astype(y_ref.dtype)


def _hgrn_sample(us, state, lb_logits, nw, *, dec_batch, td, n_heads, layer):
    rows = LANES
    assert rows % td == 0 and 8 % td == 0
    bb = rows // td
    assert dec_batch % bb == 0
    kern = functools.partial(_hgrn_sample_kernel, bb=bb, td=td, layer=layer)

    def in_spec(part):
        return pl.BlockSpec((rows, LANES), lambda bi, h: (bi, part * n_heads + h))

    return pl.pallas_call(
        kern,
        out_shape=(jax.ShapeDtypeStruct((dec_batch * td, n_heads * LANES), BF16),
                   jax.ShapeDtypeStruct(state.shape, F32)),
        grid=(dec_batch // bb, n_heads),
        in_specs=[in_spec(0), in_spec(1), in_spec(2), in_spec(3),
                  pl.BlockSpec((lb_logits.shape[0], LANES), lambda bi, h: (0, h)),
                  pl.BlockSpec((1, LANES), lambda bi, h: (0, 0)),
                  pl.BlockSpec((bb, 1, LANES, LANES), lambda bi, h: (bi, h, 0, 0))],
        out_specs=(pl.BlockSpec((rows, LANES), lambda bi, h: (bi, h)),
                   pl.BlockSpec((bb, 1, LANES, LANES), lambda bi, h: (bi, h, 0, 0))),
        scratch_shapes=[pltpu.VMEM((rows, LANES), F32)],
        compiler_params=_cparams(("parallel", "parallel")),
        name="hgrn_sample",
    )(us, us, us, us, lb_logits, nw, state)


def _attn_prompt_kernel(q_ref, k_ref, v_ref, z_ref, y_ref, m_acc, den_acc, o_acc, *, seq, scale):
    g = pl.program_id(2)
    neg_inf = -jnp.inf
    contract_last = (((1,), (1,)), ((), ()))

    def run_group(first, window, dil):
        wk = window // dil
        nb = seq // window
        ii = lax.broadcasted_iota(jnp.int32, (wk, wk), 0)
        jj = lax.broadcasted_iota(jnp.int32, (wk, wk), 1)
        mask_prev = jj >= ii
        mask_own = jj <= ii
        ones = jnp.ones((wk, LANES), BF16)

        def rows(start):
            return pl.ds(start, wk, stride=dil) if dil > 1 else pl.ds(start, wk)

        def block(idx, carry):
            n, r = idx // dil, idx % dil
            own = rows(n * window + r)
            prev = rows(jnp.maximum(n - 1, 0) * window + r)
            q = q_ref[0, own, :].astype(BF16)
            k_own = k_ref[0, own, :].astype(BF16)
            k_prev = k_ref[0, prev, :].astype(BF16)
            v_own = jnp.concatenate([v_ref[0, own, :].astype(BF16), ones], axis=1)
            v_prev = jnp.concatenate([v_ref[0, prev, :].astype(BF16), ones], axis=1)
            s_own = lax.dot_general(q, k_own, contract_last, preferred_element_type=F32) * scale
            s_prev = lax.dot_general(q, k_prev, contract_last, preferred_element_type=F32) * scale
            no_prev = jnp.where(n > 0, 0.0, neg_inf)
            s_own = jnp.where(mask_own, s_own, neg_inf)
            s_prev = jnp.where(mask_prev, s_prev + no_prev, neg_inf)
            m = jnp.max(jnp.maximum(s_own, s_prev), axis=-1, keepdims=True)
            p_own = jnp.exp(s_own - m).astype(BF16)
            p_prev = jnp.exp(s_prev - m).astype(BF16)
            acc = (jnp.dot(p_own, v_own, preferred_element_type=F32)
                   + jnp.dot(p_prev, v_prev, preferred_element_type=F32))
            o, den = acc[:, :LANES], acc[:, LANES:]
            if first:
                m_acc[own, :] = jnp.broadcast_to(m, (wk, LANES))
                den_acc[own, :] = den
                o_acc[own, :] = o
            else:
                m_old = m_acc[own, :]
                m_new = jnp.maximum(m_old, m)
                a_old = jnp.exp(m_old - m_new)
                a_new = jnp.exp(m - m_new)
                m_acc[own, :] = m_new
                den_acc[own, :] = den_acc[own, :] * a_old + den * a_new
                o_acc[own, :] = o_acc[own, :] * a_old + o * a_new
            return carry

        lax.fori_loop(0, nb * dil, block, 0, unroll=8)

    for gi, (window, dil) in enumerate(DIL_GROUPS):
        @pl.when(g == N_GROUPS - 1 - gi)
        def _(gi=gi, window=window, dil=dil):
            run_group(gi == N_GROUPS - 1, window, dil)

    @pl.when(g == N_GROUPS - 1)
    def _():
        y = (o_acc[...] / den_acc[...]) * _silu(z_ref[0])
        y_ref[...] = y.astype(y_ref.dtype)


def _attn_prompt(u3, *, batch, seq, h_g, q_off, k_off, v_off, z_off):
    for window, _ in DIL_GROUPS:
        assert seq % window == 0
    kern = functools.partial(_attn_prompt_kernel, seq=seq, scale=float(LANES) ** -0.5)

    def in_spec(off):
        return pl.BlockSpec((1, seq, LANES),
                            lambda b, h, g: (off + (N_GROUPS - 1 - g) * h_g + h, b, 0))

    return pl.pallas_call(
        kern,
        out_shape=jax.ShapeDtypeStruct((batch * seq, h_g * LANES), BF16),
        grid=(batch, h_g, N_GROUPS),
        in_specs=[in_spec(q_off), in_spec(k_off), in_spec(v_off),
                  pl.BlockSpec((1, seq, LANES), lambda b, h, g: (z_off + h, b, 0))],
        out_specs=pl.BlockSpec((seq, LANES), lambda b, h, g: (b, h)),
        scratch_shapes=[pltpu.VMEM((seq, LANES), F32)] * 3,
        compiler_params=_cparams(("parallel", "parallel", "arbitrary")),
        name="attn_prompt",
    )(u3, u3, u3, u3)


def _attn_sample_kernel(uq_ref, c0_ref, c1_ref, c2_ref, y_ref, *, td, h_g):
    def element(e, carry):
        _attn_sample_element(uq_ref, (c0_ref, c1_ref, c2_ref), y_ref, e, td=td, h_g=h_g)
        return carry

    lax.fori_loop(0, uq_ref.shape[0] // td, element, 0)


def _attn_sample_element(uq_ref, c_refs, y_ref, e, *, td, h_g):
    ro = e * td
    qscale = float(LANES) ** -0.5 * 1.4426950408889634
    neg_inf = -jnp.inf
    contract_last = (((1,), (1,)), ((), ()))
    q_part, k_part, v_part, z_part = 0, N_GROUPS, 2 * N_GROUPS, 3 * N_GROUPS
    wk = DIL_GROUPS[0][0] // DIL_GROUPS[0][1]
    nl = N_GROUPS * h_g
    shape3 = (wk, h_g, nl)
    c_row = lax.broadcasted_iota(jnp.int32, shape3, 0)
    key_head = lax.broadcasted_iota(jnp.int32, shape3, 1)
    lane = lax.broadcasted_iota(jnp.int32, shape3, 2)
    head_match = lane % h_g == key_head
    e_row = lax.broadcasted_iota(jnp.int32, (nl, LANES), 0)
    expand = [jnp.where(e_row // h_g == g, 1.0, 0.0).astype(BF16) for g in range(N_GROUPS)]
    zeros_q = jnp.zeros((h_g, LANES), F32)

    for i in range(td):
        r = ro + i
        valid = head_match
        k_parts, q_rows, v_cs, s_new, v_new = [], [], [], [], []
        for g, (window, dil) in enumerate(DIL_GROUPS):
            assert window // dil == wk
            c_ref = c_refs[g]
            q_t = uq_ref[r, q_part + g]
            if dil == 1:
                k_c, v_c = c_ref[e, :, 0], c_ref[e, :, 1]
                new_js = tuple(range(i + 1))
                valid = valid & ((lane // h_g != g) | (c_row >= i))
            else:
                k_c, v_c = c_ref[e, :, i, 0], c_ref[e, :, i, 1]
                new_js = (i,)
            k_parts.append(k_c.reshape(wk * h_g, LANES).astype(BF16))
            q_rows.append(jnp.concatenate(
                [q_t * qscale if gg == g else zeros_q for gg in range(N_GROUPS)], axis=1))
            v_cs.append(v_c)
            for jn in new_js:
                s_new.append(jnp.sum(uq_ref[ro + jn, k_part + g] * q_t, axis=-1, keepdims=True) * qscale)
                v_new.append(uq_ref[ro + jn, v_part + g])
        s = lax.dot_general(jnp.concatenate(k_parts, axis=1),
                            jnp.concatenate(q_rows, axis=0).astype(BF16),
                            contract_last, preferred_element_type=F32)
        s = jnp.where(valid, s.reshape(shape3), neg_inf)
        m = jnp.max(jnp.max(s, axis=0), axis=-1, keepdims=True)
        for sn in s_new:
            m = jnp.maximum(m, sn)
        p = jnp.exp2(s - m[None])
        den = jnp.sum(jnp.sum(p, axis=0), axis=-1, keepdims=True)
        p_bf = p.astype(BF16).reshape(wk * h_g, nl)
        y = jnp.zeros((h_g, LANES), F32)
        for g in range(N_GROUPS):
            pb = jnp.dot(p_bf, expand[g], preferred_element_type=F32)
            y = y + jnp.sum(pb.reshape(wk, h_g, LANES) * v_cs[g], axis=0)
        for sn, vn in zip(s_new, v_new):
            pn = jnp.exp2(sn - m)
            den = den + pn
            y = y + pn * vn
        y_ref[r] = (y / den) * _silu(uq_ref[r, z_part])


def _attn_sample(uq, caches, *, dec_batch, td, h_g):
    rows_blk = 8
    assert rows_blk % td == 0 and (dec_batch * td) % rows_blk == 0
    per_blk = rows_blk // td
    parts = uq.shape[1]
    views, c_specs = [], []
    for (window, dil), c in zip(DIL_GROUPS, caches):
        assert c.shape[1] == window, "window buffers must hold a full window"
        wk = window // dil
        if dil == 1:
            views.append(c)
            c_specs.append(pl.BlockSpec((per_blk, wk, 2, h_g, LANES), lambda s: (s, 0, 0, 0, 0)))
        else:
            assert td <= dil
            views.append(c.reshape(dec_batch, wk, dil, 2, h_g, LANES))
            c_specs.append(pl.BlockSpec((per_blk, wk, td, 2, h_g, LANES), lambda s: (s, 0, 0, 0, 0, 0)))
    kern = functools.partial(_attn_sample_kernel, td=td, h_g=h_g)
    return pl.pallas_call(
        kern,
        out_shape=jax.ShapeDtypeStruct((dec_batch * td, h_g, LANES), F32),
        grid=(dec_batch // per_blk,),
        in_specs=[pl.BlockSpec((rows_blk, parts, h_g, LANES), lambda s: (s, 0, 0, 0))] + c_specs,
        out_specs=pl.BlockSpec((rows_blk, h_g, LANES), lambda s: (s, 0, 0)),
        compiler_params=_cparams(("arbitrary",)),
        name="attn_sample",
    )(uq, *views)


def _merge_kernel(ya_ref, yb_ref, wa_ref, wb_ref, ga_ref, gb_ref, o_ref, *, nc, blocked_gates):
    pa = jnp.dot(ya_ref[...].astype(BF16), wa_ref[...], preferred_element_type=F32)
    pb = jnp.dot(yb_ref[...].astype(BF16), wb_ref[...], preferred_element_type=F32)
    for c in range(nc):
        ls = slice(c * LANES, (c + 1) * LANES)
        ga = ga_ref[c] if blocked_gates else ga_ref[:, ls]
        gb = gb_ref[c] if blocked_gates else gb_ref[:, ls]
        o_ref[:, ls] = (_sigmoid(ga) * pa[:, ls] + _sigmoid(gb) * pb[:, ls]).astype(o_ref.dtype)


def _merge(ya, yb, wa_bf, wb_bf, gates, *, ga_off, gb_off, blocked_gates):
    m = ya.shape[0]
    d = wa_bf.shape[1]
    tm = ROW_TILE
    tn = _largest_divisor(d, (1024, 512, 256, 128))
    nc = tn // LANES
    assert (ga_off * LANES) % tn == 0 and (gb_off * LANES) % tn == 0
    ga_blk, gb_blk = ga_off * LANES // tn, gb_off * LANES // tn
    if blocked_gates:
        ga_spec = pl.BlockSpec((nc, tm, LANES), lambda j, i: (ga_blk + j, i, 0))
        gb_spec = pl.BlockSpec((nc, tm, LANES), lambda j, i: (gb_blk + j, i, 0))
    else:
        ga_spec = pl.BlockSpec((tm, tn), lambda j, i: (i, ga_blk + j))
        gb_spec = pl.BlockSpec((tm, tn), lambda j, i: (i, gb_blk + j))
    kern = functools.partial(_merge_kernel, nc=nc, blocked_gates=blocked_gates)
    return pl.pallas_call(
        kern,
        out_shape=jax.ShapeDtypeStruct((m, d), BF16),
        grid=(d // tn, m // tm),
        in_specs=[pl.BlockSpec((tm, ya.shape[1]), lambda j, i: (i, 0)),
                  pl.BlockSpec((tm, yb.shape[1]), lambda j, i: (i, 0)),
                  pl.BlockSpec((wa_bf.shape[0], tn), lambda j, i: (0, j)),
                  pl.BlockSpec((wb_bf.shape[0], tn), lambda j, i: (0, j)),
                  ga_spec, gb_spec],
        out_specs=pl.BlockSpec((tm, tn), lambda j, i: (i, j)),
        compiler_params=_cparams(("parallel", "parallel")),
        name="merge_blocked" if blocked_gates else "merge_rows",
    )(ya, yb, wa_bf, wb_bf, gates, gates)


def _outproj_kernel(mg_ref, w_ref, x_ref, fw_ref, y_ref, *, tn):
    d = y_ref.shape[1]
    mg = mg_ref[...]
    for c in range(d // tn):
        cols = slice(c * tn, (c + 1) * tn)
        y_ref[:, cols] = x_ref[:, cols] + jnp.dot(mg, w_ref[:, cols], preferred_element_type=F32)

    rc = 64
    fw = fw_ref[...]

    def norm_rows(c, carry):
        rows = pl.ds(pl.multiple_of(c * rc, rc), rc)
        full = y_ref[rows, :]
        ms = jnp.mean(full * full, axis=-1, keepdims=True)
        y_ref[rows, :] = full * lax.rsqrt(ms + RMS_EPS) * fw
        return carry

    lax.fori_loop(0, y_ref.shape[0] // rc, norm_rows, 0)


def _outproj(merged, w_bf, x, fw):
    m, d = x.shape
    tm = ROW_TILE // 2
    tn = _largest_divisor(d, (1024, 512, 256, 128))
    kern = functools.partial(_outproj_kernel, tn=tn)
    return pl.pallas_call(
        kern,
        out_shape=jax.ShapeDtypeStruct((m, d), F32),
        grid=(m // tm,),
        in_specs=[pl.BlockSpec((tm, d), lambda i: (i, 0)),
                  pl.BlockSpec((d, d), lambda i: (0, 0), pipeline_mode=pl.Buffered(1)),
                  pl.BlockSpec((tm, d), lambda i: (i, 0)),
                  pl.BlockSpec((1, d), lambda i: (0, 0))],
        out_specs=pl.BlockSpec((tm, d), lambda i: (i, 0)),
        compiler_params=_cparams(("arbitrary",)),
        name="outproj_norm",
    )(merged, w_bf, x, fw.reshape(1, d))


def kernel(x_prompt, x_sample, cache_kv_w128, cache_kv_w512, cache_kv_w2048, state_hgrn,
           norm_w, w_in, lb_logits, hgrn_norm_w, w_proj_a, w_proj_b, w_out, final_norm_w):
    depth = norm_w.shape[0]
    assert depth == 1, "single-layer step"
    layer = 0
    batch, seq, d = x_prompt.shape
    dec_batch, td, _ = x_sample.shape
    w_a = w_proj_a.shape[1]
    w_bo = w_proj_b.shape[1]
    n_heads = w_a // LANES
    h_g = w_bo // LANES
    n_b = N_GROUPS * h_g
    n_d = d // LANES
    qb_off = 4 * n_heads
    kb_off, vb_off = qb_off + n_b, qb_off + 2 * n_b
    zb_off = qb_off + 3 * n_b
    ga_off = zb_off + h_g
    gb_off = ga_off + n_d
    assert (gb_off + n_d) * LANES == w_in.shape[2]

    xp = x_prompt.reshape(batch * seq, d)
    xs = x_sample.reshape(dec_batch * td, d)
    caches = (cache_kv_w128[layer], cache_kv_w512[layer], cache_kv_w2048[layer])

    wa_bf = _cast_bf16(w_proj_a[layer])
    wb_bf = _cast_bf16(w_proj_b[layer])
    wo_bf = _cast_bf16(w_out[layer])

    h_p = _norm_cast(xp, norm_w[layer])
    h_s = _norm_cast(xs, norm_w[layer])
    u3, us, uq, *kv_p = _inproj(h_p, h_s, w_in[layer], batch=batch, seq=seq, h_g=h_g,
                                attn_off=qb_off, attn_parts=3 * N_GROUPS + 1)
    kv_p = [kv.reshape(1, batch, kv.shape[0] // batch, 2, h_g, LANES) for kv in kv_p]

    nw = hgrn_norm_w[layer].reshape(1, LANES)
    ya_p, s_p = _hgrn_prompt(u3, lb_logits, nw, batch=batch, seq=seq, n_heads=n_heads, layer=layer)
    ya_s, s_s = _hgrn_sample(us, state_hgrn[layer], lb_logits, nw,
                             dec_batch=dec_batch, td=td, n_heads=n_heads, layer=layer)

    yb_p = _attn_prompt(u3, batch=batch, seq=seq, h_g=h_g,
                        q_off=qb_off, k_off=kb_off, v_off=vb_off, z_off=zb_off)
    yb_s = _attn_sample(uq, caches, dec_batch=dec_batch, td=td, h_g=h_g)
    yb_s = yb_s.reshape(dec_batch * td, h_g * LANES)

    mg_p = _merge(ya_p, yb_p, wa_bf, wb_bf, u3, ga_off=ga_off, gb_off=gb_off, blocked_gates=True)
    mg_s = _merge(ya_s, yb_s, wa_bf, wb_bf, us, ga_off=ga_off, gb_off=gb_off, blocked_gates=False)
    y_p = _outproj(mg_p, wo_bf, xp, final_norm_w)
    y_s = _outproj(mg_s, wo_bf, xs, final_norm_w)

    kv_s = [jnp.stack([uq[:, N_GROUPS + g], uq[:, 2 * N_GROUPS + g]], axis=1)
            .reshape(1, dec_batch, td, 2, h_g, LANES) for g in range(N_GROUPS)]

    return (y_p.reshape(batch, seq, d), y_s.reshape(dec_batch, td, d),
            kv_p[0], kv_p[1], kv_p[2], s_p[None],
            kv_s[0], kv_s[1], kv_s[2], s_s[None])
```

```python
import functools

import jax
import jax.numpy as jnp
from jax import lax
from jax.experimental import pallas as pl
from jax.experimental.pallas import tpu as pltpu

F32 = jnp.float32
BF16 = jnp.bfloat16

LANES = 128
RMS_EPS = 1e-6
HGRN_CHUNK = 64
DIL_GROUPS = ((128, 1), (512, 4), (2048, 16))
N_GROUPS = len(DIL_GROUPS)
ROW_TILE = 512
VMEM_LIMIT = 56 * 1024 * 1024


def _cparams(semantics, vmem=VMEM_LIMIT):
    return pltpu.CompilerParams(dimension_semantics=semantics, vmem_limit_bytes=vmem)


def _sigmoid(x):
    return 1.0 / (1.0 + jnp.exp(-x))


def _silu(x):
    return x * _sigmoid(x)


def _largest_divisor(n, candidates):
    for c in candidates:
        if n % c == 0:
            return c
    raise ValueError(f"no tile in {candidates} divides {n}")


def _norm_cast_kernel(x_ref, w_ref, o_ref):
    x = x_ref[...]
    ms = jnp.mean(x * x, axis=-1, keepdims=True)
    o_ref[...] = (x * lax.rsqrt(ms + RMS_EPS) * w_ref[...]).astype(o_ref.dtype)


def _norm_cast(x, w):
    m, d = x.shape
    tm = _largest_divisor(m, (ROW_TILE, 256, 128))
    return pl.pallas_call(
        _norm_cast_kernel,
        out_shape=jax.ShapeDtypeStruct((m, d), BF16),
        grid=(m // tm,),
        in_specs=[pl.BlockSpec((tm, d), lambda i: (i, 0)),
                  pl.BlockSpec((1, d), lambda i: (0, 0))],
        out_specs=pl.BlockSpec((tm, d), lambda i: (i, 0)),
        compiler_params=_cparams(("parallel",)),
        name="norm_cast",
    )(x, w.reshape(1, d))


def _cast_kernel(x_ref, o_ref):
    o_ref[...] = x_ref[...].astype(o_ref.dtype)


def _cast_bf16(w):
    r, c = w.shape
    tr = _largest_divisor(r, (512, 256, 128))
    tc = _largest_divisor(c, (2048, 1024, 512, 256, 128))
    return pl.pallas_call(
        _cast_kernel,
        out_shape=jax.ShapeDtypeStruct((r, c), BF16),
        grid=(r // tr, c // tc),
        in_specs=[pl.BlockSpec((tr, tc), lambda i, j: (i, j))],
        out_specs=pl.BlockSpec((tr, tc), lambda i, j: (i, j)),
        compiler_params=_cparams(("parallel", "parallel")),
        name="cast_bf16",
    )(w)


def _inproj_kernel(hp_ref, hs_ref, w_hbm, u3_ref, us_ref, uq_ref, *rest,
                   n_p, nc, h_g, j_lo, j_hi, kv_plan, tiles_per_seq, n_chunks):
    kv_refs = rest[:len(kv_plan)]
    wbf_ref, stage_ref, sem = rest[len(kv_plan):]
    j = pl.program_id(0)
    i = pl.program_id(1)
    _, kc, tn = stage_ref.shape
    cur = j % 2
    nxt = 1 - cur

    def chunk_copy(tile, c, slot):
        return pltpu.make_async_copy(
            w_hbm.at[pl.ds(pl.multiple_of(c * kc, kc), kc), pl.ds(pl.multiple_of(tile * tn, tn), tn)],
            stage_ref.at[slot], sem.at[slot])

    def cast_chunk(c, slot, buf):
        wbf_ref[buf, pl.ds(pl.multiple_of(c * kc, kc), kc), :] = stage_ref[slot].astype(BF16)

    @pl.when((j == 0) & (i == 0))
    def _():
        def first_tile(c, carry):
            cp = chunk_copy(0, c, c % 2)
            cp.start()
            cp.wait()
            cast_chunk(c, c % 2, 0)
            return carry
        lax.fori_loop(0, n_chunks, first_tile, 0)

    has_next = j + 1 < pl.num_programs(0)

    @pl.when(has_next & (i >= 1) & (i <= n_chunks))
    def _():
        chunk_copy(j + 1, i - 1, (i + 1) % 2).wait()

    @pl.when(has_next & (i < n_chunks))
    def _():
        chunk_copy(j + 1, i, i % 2).start()

    def cast_arrived_chunk():
        cast_chunk(jnp.clip(i - 1, 0, n_chunks - 1), (i + 1) % 2, nxt)

    @pl.when(i < n_p)
    def _():
        cast_arrived_chunk()
        acc = jnp.dot(hp_ref[...], wbf_ref[cur], preferred_element_type=F32)
        for c in range(nc):
            u3_ref[c] = acc[:, c * LANES:(c + 1) * LANES]

        tile_in_seq = i % tiles_per_seq
        for kv_ref, (jk, ck, jv, cv, n_keep_tiles) in zip(kv_refs, kv_plan):
            keep_rows = kv_ref.shape[0]
            for jw, c0 in ((jk, ck), (jv, cv)):
                @pl.when((j == jw) & (tile_in_seq >= tiles_per_seq - n_keep_tiles))
                def _(kv_ref=kv_ref, c0=c0, keep_rows=keep_rows):
                    r0 = acc.shape[0] - keep_rows
                    flat = kv_ref.reshape(keep_rows * h_g, LANES)
                    for h in range(h_g):
                        flat[pl.ds(h, keep_rows, stride=h_g), :] = (
                            acc[r0:, (c0 + h) * LANES:(c0 + h + 1) * LANES])

    @pl.when(i >= n_p)
    def _():
        cast_arrived_chunk()
        acc = jnp.dot(hs_ref[...], wbf_ref[cur], preferred_element_type=F32)
        us_ref[...] = acc

        @pl.when((j >= j_lo) & (j < j_hi))
        def _():
            rows, parts = uq_ref.shape[0], uq_ref.shape[1]
            flat = uq_ref.reshape(rows * parts * h_g, LANES)
            for c in range(nc):
                flat[pl.ds(c, rows, stride=parts * h_g), :] = acc[:, c * LANES:(c + 1) * LANES]


def _inproj(h_p, h_s, w_f32, *, batch, seq, h_g, attn_off, attn_parts):
    mp, d = h_p.shape
    ms = h_s.shape[0]
    n = w_f32.shape[1]
    tm = ROW_TILE
    tn = _largest_divisor(n, (1024, 512, 256, 128))
    n_p, n_s = mp // tm, ms // tm
    n_chunks = n_p + n_s - 1
    assert d % n_chunks == 0 and (d // n_chunks) % 16 == 0
    kc = d // n_chunks
    nc = tn // LANES
    assert nc % h_g == 0 and (attn_off * LANES) % tn == 0 and (attn_parts * h_g) % nc == 0
    ppt = nc // h_g
    j_lo = attn_off * LANES // tn
    n_aj = attn_parts // ppt
    assert seq % tm == 0
    tiles_per_seq = seq // tm

    kv_plan, kv_shapes, kv_specs = [], [], []
    for g, (window, _) in enumerate(DIL_GROUPS):
        keep = min(window, seq)
        blk_rows = min(keep, tm)
        assert keep % blk_rows == 0 and seq % keep == 0
        nkt = keep // blk_rows
        k_part, v_part = N_GROUPS + g, 2 * N_GROUPS + g
        jk, ck = j_lo + k_part // ppt, (k_part % ppt) * h_g
        jv, cv = j_lo + v_part // ppt, (v_part % ppt) * h_g
        assert jk < jv
        kv_plan.append((jk, ck, jv, cv, nkt))
        kv_shapes.append(jax.ShapeDtypeStruct((batch * keep, 2, h_g, LANES), F32))
        last = batch * nkt - 1

        def kv_index(j, i, jk=jk, jv=jv, nkt=nkt, last=last):
            ip = jnp.minimum(i, n_p - 1)
            moving = (ip // tiles_per_seq) * nkt + jnp.clip(ip % tiles_per_seq - (tiles_per_seq - nkt), 0, nkt - 1)
            rb = jnp.where(j < jk, 0, jnp.where((j == jk) | (j == jv), moving, last))
            return (rb, (j >= jv).astype(jnp.int32), 0, 0)

        kv_specs.append(pl.BlockSpec((blk_rows, 1, h_g, LANES), kv_index))

    kern = functools.partial(_inproj_kernel, n_p=n_p, nc=nc, h_g=h_g, j_lo=j_lo, j_hi=j_lo + n_aj,
                             kv_plan=tuple(kv_plan), tiles_per_seq=tiles_per_seq, n_chunks=n_chunks)
    return pl.pallas_call(
        kern,
        out_shape=(jax.ShapeDtypeStruct((n // LANES, mp, LANES), F32),
                   jax.ShapeDtypeStruct((ms, n), F32),
                   jax.ShapeDtypeStruct((ms, attn_parts, h_g, LANES), F32),
                   *kv_shapes),
        grid=(n // tn, n_p + n_s),
        in_specs=[
            pl.BlockSpec((tm, d), lambda j, i: (jnp.minimum(i, n_p - 1), 0)),
            pl.BlockSpec((tm, d), lambda j, i: (jnp.clip(i - n_p, 0, n_s - 1), 0),
                         pipeline_mode=pl.Buffered(1) if n_s == 1 else None),
            pl.BlockSpec(memory_space=pl.ANY),
        ],
        out_specs=(
            pl.BlockSpec((nc, tm, LANES), lambda j, i: (j, jnp.minimum(i, n_p - 1), 0)),
            pl.BlockSpec((tm, tn), lambda j, i: (jnp.clip(i - n_p, 0, n_s - 1), j)),
            pl.BlockSpec((tm, ppt, h_g, LANES),
                         lambda j, i: (jnp.clip(i - n_p, 0, n_s - 1), jnp.clip(j - j_lo, 0, n_aj - 1), 0, 0)),
            *kv_specs,
        ),
        scratch_shapes=[pltpu.VMEM((2, d, tn), BF16),
                        pltpu.VMEM((2, kc, tn), F32),
                        pltpu.SemaphoreType.DMA((2,))],
        compiler_params=_cparams(("arbitrary", "arbitrary")),
        name="inproj",
    )(h_p, h_s, w_f32)


def _lower_bound(lb_logits, layer):
    mx = jnp.max(lb_logits, axis=0, keepdims=True)
    e = jnp.exp(lb_logits - mx)
    den = jnp.sum(e, axis=0, keepdims=True)
    num = e[0:1]
    for r in range(1, layer + 1):
        num = num + e[r:r + 1]
    return num / den


def _split3_bf16(x):
    hi = x.astype(BF16)
    r1 = x - hi.astype(F32)
    mid = r1.astype(BF16)
    lo = (r1 - mid.astype(F32)).astype(BF16)
    return hi, mid, lo


def _hgrn_prompt_kernel(q_ref, f_ref, i_ref, z_ref, lb_ref, nw_ref, y_ref, sfin_ref, st_ref,
                        *, hb, tb, chunk, cpb, layer):
    t = pl.program_id(2)

    @pl.when(t == 0)
    def _():
        st_ref[...] = jnp.zeros_like(st_ref)

    lb_all = _lower_bound(lb_ref[...], layer)
    nw = nw_ref[...]
    rb = cpb * chunk
    row = lax.broadcasted_iota(jnp.int32, (rb, rb), 0)
    col = lax.broadcasted_iota(jnp.int32, (rb, rb), 1)
    causal = (row >= col) & (row // chunk == col // chunk)
    tril = jnp.where(causal, 1.0, 0.0).astype(BF16)
    dk = q_ref.shape[-1]
    qscale = float(dk) ** -0.5
    contract_last = (((1,), (1,)), ((), ()))
    contract_first = (((0,), (0,)), ((), ()))

    def block_body(blk, carry):
        r0 = pl.multiple_of(blk * rb, rb)
        for hl in range(hb):
            lb = lb_all[:, hl * LANES:(hl + 1) * LANES]
            qa = q_ref[hl, pl.ds(r0, rb), :]
            fa = f_ref[hl, pl.ds(r0, rb), :]
            v = i_ref[hl, pl.ds(r0, rb), :]
            za = z_ref[hl, pl.ds(r0, rb), :]
            q = _silu(qa) * qscale
            f = lb + (1.0 - lb) * _sigmoid(fa)
            log_f = jnp.log(f)
            k = 1.0 - f
            hi, mid, lo = _split3_bf16(log_f)
            parts = jnp.dot(tril, jnp.concatenate([hi, mid, lo], axis=1),
                            preferred_element_type=F32)
            g = (parts[:, 0:LANES] + parts[:, LANES:2 * LANES]) + parts[:, 2 * LANES:3 * LANES]
            g_last = [g[(c + 1) * chunk - 1:(c + 1) * chunk, :] for c in range(cpb)]
            g_last_rows = jnp.concatenate(
                [jnp.broadcast_to(gl, (chunk, LANES)) for gl in g_last], axis=0)
            q_dec = (q * jnp.exp(g)).astype(BF16)
            k_rel = (k * jnp.exp(-g)).astype(BF16)
            k_end = (k * jnp.exp(g_last_rows - g)).astype(BF16)
            vb = v.astype(BF16)
            a = lax.dot_general(q_dec, k_rel, contract_last, preferred_element_type=F32)
            a = jnp.where(causal, a, 0.0)
            o_intra = jnp.dot(a.astype(BF16), vb, preferred_element_type=F32)
            st = st_ref[hl]
            o_inter = []
            for c in range(cpb):
                rows = slice(c * chunk, (c + 1) * chunk)
                o_inter.append(lax.dot_general(q_dec[rows], st.astype(BF16), contract_last,
                                               preferred_element_type=F32))
                ds_t = lax.dot_general(vb[rows], k_end[rows], contract_first,
                                       preferred_element_type=F32)
                st = jnp.exp(g_last[c]) * st + ds_t
            st_ref[hl] = st
            o = o_intra + jnp.concatenate(o_inter, axis=0)
            ms = jnp.mean(o * o, axis=-1, keepdims=True)
            y = (o * lax.rsqrt(ms + RMS_EPS) * nw) * _silu(za)
            y_ref[pl.ds(r0, rb), hl * LANES:(hl + 1) * LANES] = y.astype(y_ref.dtype)
        return carry

    lax.fori_loop(0, tb // rb, block_body, 0)

    @pl.when(t == pl.num_programs(2) - 1)
    def _():
        for hl in range(hb):
            sfin_ref[0, hl] = st_ref[hl].T


def _hgrn_prompt(u3, lb_logits, nw, *, batch, seq, n_heads, layer):
    hb = _largest_divisor(n_heads, (8, 4, 2, 1))
    tb = _largest_divisor(seq, (1024, 512, 256, 128, 64))
    chunk = HGRN_CHUNK
    assert tb % chunk == 0
    nt = seq // tb
    nhb = n_heads // hb
    cpb = _largest_divisor(tb // chunk, (4, 2, 1))
    kern = functools.partial(_hgrn_prompt_kernel, hb=hb, tb=tb, chunk=chunk, cpb=cpb, layer=layer)

    def in_spec(part):
        return pl.BlockSpec((hb, tb, LANES), lambda b, h, t: (part * nhb + h, b * nt + t, 0))

    return pl.pallas_call(
        kern,
        out_shape=(jax.ShapeDtypeStruct((batch * seq, n_heads * LANES), BF16),
                   jax.ShapeDtypeStruct((batch, n_heads, LANES, LANES), F32)),
        grid=(batch, nhb, nt),
        in_specs=[in_spec(0), in_spec(1), in_spec(2), in_spec(3),
                  pl.BlockSpec((lb_logits.shape[0], hb * LANES), lambda b, h, t: (0, h)),
                  pl.BlockSpec((1, LANES), lambda b, h, t: (0, 0))],
        out_specs=(pl.BlockSpec((tb, hb * LANES), lambda b, h, t: (b * nt + t, h)),
                   pl.BlockSpec((1, hb, LANES, LANES), lambda b, h, t: (b, h, 0, 0))),
        scratch_shapes=[pltpu.VMEM((hb, LANES, LANES), F32)],
        compiler_params=_cparams(("parallel", "parallel", "arbitrary")),
        name="hgrn_prompt",
    )(u3, u3, u3, u3, lb_logits, nw)


def _hgrn_sample_kernel(q_ref, f_ref, i_ref, z_ref, lb_ref, nw_ref, s_ref, y_ref, so_ref, o_scr,
                        *, bb, td, layer):
    lb = _lower_bound(lb_ref[...], layer)
    dk = q_ref.shape[-1]
    q = _silu(q_ref[...]) * (float(dk) ** -0.5)
    f = lb + (1.0 - lb) * _sigmoid(f_ref[...])
    k = 1.0 - f
    v = i_ref[...]
    rows = bb * td
    sub = 8
    contract_last = (((1,), (1,)), ((), ()))
    contract_first = (((0,), (0,)), ((), ()))
    t_idx = lax.broadcasted_iota(jnp.int32, (rows, LANES), 0) % td
    p_inc = f
    s = 1
    while s < td:
        p_inc = p_inc * jnp.where(t_idx >= s, pltpu.roll(p_inc, s, axis=0), 1.0)
        s *= 2
    e_suf = jnp.where(t_idx + 1 < td, pltpu.roll(f, rows - 1, axis=0), 1.0)
    s = 1
    while s < td:
        e_suf = e_suf * jnp.where(t_idx + s < td, pltpu.roll(e_suf, rows - s, axis=0), 1.0)
        s *= 2
    q_dec = q * p_inc
    k_end = k * e_suf
    a = lax.dot_general(q_dec.astype(BF16), (k / p_inc).astype(BF16), contract_last,
                        preferred_element_type=F32)
    ri = lax.broadcasted_iota(jnp.int32, (rows, rows), 0)
    ci = lax.broadcasted_iota(jnp.int32, (rows, rows), 1)
    a = jnp.where((ri // td == ci // td) & (ci <= ri), a, 0.0)
    o_intra = jnp.dot(a.astype(BF16), v.astype(BF16), preferred_element_type=F32)
    dec_t = p_inc.T
    row8 = lax.broadcasted_iota(jnp.int32, (sub, LANES), 0)
    for grp in range(rows // sub):
        sl = slice(grp * sub, (grp + 1) * sub)
        qd8 = q_dec[sl].astype(BF16)
        o8 = jnp.zeros((sub, LANES), F32)
        for u in range(sub // td):
            bl = grp * (sub // td) + u
            mine = (row8 >= u * td) & (row8 < (u + 1) * td)
            s0 = s_ref[bl, 0]
            o8 = jnp.where(mine, jnp.dot(qd8, s0.astype(BF16), preferred_element_type=F32), o8)
            ds = lax.dot_general(jnp.where(mine, k_end[sl], 0.0).astype(BF16),
                                 jnp.where(mine, v[sl], 0.0).astype(BF16), contract_first,
                                 preferred_element_type=F32)
            j = bl * td + td - 1
            so_ref[bl, 0] = dec_t[:, j:j + 1] * s0 + ds
        o_scr[sl, :] = o8
    o = o_intra + o_scr[...]
    ms = jnp.mean(o * o, axis=-1, keepdims=True)
    y = (o * lax.rsqrt(ms + RMS_EPS) * nw_ref[...]) * _silu(z_ref[...])
    y_ref[...] = y.astype(y_ref.dtype)


def _hgrn_sample(us, state, lb_logits, nw, *, dec_batch, td, n_heads, layer):
    rows = LANES
    assert rows % td == 0 and 8 % td == 0
    bb = rows // td
    assert dec_batch % bb == 0
    kern = functools.partial(_hgrn_sample_kernel, bb=bb, td=td, layer=layer)

    def in_spec(part):
        return pl.BlockSpec((rows, LANES), lambda bi, h: (bi, part * n_heads + h))

    return pl.pallas_call(
        kern,
        out_shape=(jax.ShapeDtypeStruct((dec_batch * td, n_heads * LANES), BF16),
                   jax.ShapeDtypeStruct(state.shape, F32)),
        grid=(dec_batch // bb, n_heads),
        in_specs=[in_spec(0), in_spec(1), in_spec(2), in_spec(3),
                  pl.BlockSpec((lb_logits.shape[0], LANES), lambda bi, h: (0, h)),
                  pl.BlockSpec((1, LANES), lambda bi, h: (0, 0)),
                  pl.BlockSpec((bb, 1, LANES, LANES), lambda bi, h: (bi, h, 0, 0))],
        out_specs=(pl.BlockSpec((rows, LANES), lambda bi, h: (bi, h)),
                   pl.BlockSpec((bb, 1, LANES, LANES), lambda bi, h: (bi, h, 0, 0))),
        scratch_shapes=[pltpu.VMEM((rows, LANES), F32)],
        compiler_params=_cparams(("parallel", "parallel")),
        name="hgrn_sample",
    )(us, us, us, us, lb_logits, nw, state)


def _attn_prompt_kernel(q_ref, k_ref, v_ref, z_ref, y_ref, m_acc, den_acc, o_acc, *, seq, scale):
    g = pl.program_id(2)
    neg_inf = -jnp.inf
    contract_last = (((1,), (1,)), ((), ()))

    def run_group(first, window, dil):
        wk = window // dil
        nb = seq // window
        ii = lax.broadcasted_iota(jnp.int32, (wk, wk), 0)
        jj = lax.broadcasted_iota(jnp.int32, (wk, wk), 1)
        mask_prev = jj >= ii
        mask_own = jj <= ii
        ones = jnp.ones((wk, LANES), BF16)

        def rows(start):
            return pl.ds(start, wk, stride=dil) if dil > 1 else pl.ds(start, wk)

        def block(idx, carry):
            n, r = idx // dil, idx % dil
            own = rows(n * window + r)
            prev = rows(jnp.maximum(n - 1, 0) * window + r)
            q = (q_ref[0, own, :] * scale).astype(BF16)
            k_own = k_ref[0, own, :].astype(BF16)
            k_prev = k_ref[0, prev, :].astype(BF16)
            v_own = jnp.concatenate([v_ref[0, own, :].astype(BF16), ones], axis=1)
            v_prev = jnp.concatenate([v_ref[0, prev, :].astype(BF16), ones], axis=1)
            s_own = lax.dot_general(q, k_own, contract_last, preferred_element_type=F32)
            s_prev = lax.dot_general(q, k_prev, contract_last, preferred_element_type=F32)
            no_prev = jnp.where(n > 0, 0.0, neg_inf)
            s_own = jnp.where(mask_own, s_own, neg_inf)
            s_prev = jnp.where(mask_prev, s_prev + no_prev, neg_inf)
            m = jnp.max(jnp.maximum(s_own, s_prev), axis=-1, keepdims=True)
            p_own = jnp.exp2(s_own - m).astype(BF16)
            p_prev = jnp.exp2(s_prev - m).astype(BF16)
            acc = (jnp.dot(p_own, v_own, preferred_element_type=F32)
                   + jnp.dot(p_prev, v_prev, preferred_element_type=F32))
            o, den = acc[:, :LANES], acc[:, LANES:]
            if first:
                m_acc[own, :] = jnp.broadcast_to(m, (wk, LANES))
                den_acc[own, :] = den
                o_acc[own, :] = o
            else:
                m_old = m_acc[own, :]
                m_new = jnp.maximum(m_old, m)
                a_old = jnp.exp2(m_old - m_new)
                a_new = jnp.exp2(m - m_new)
                m_acc[own, :] = m_new
                den_acc[own, :] = den_acc[own, :] * a_old + den * a_new
                o_acc[own, :] = o_acc[own, :] * a_old + o * a_new
            return carry

        lax.fori_loop(0, nb * dil, block, 0, unroll=8)

    for gi, (window, dil) in enumerate(DIL_GROUPS):
        @pl.when(g == N_GROUPS - 1 - gi)
        def _(gi=gi, window=window, dil=dil):
            run_group(gi == N_GROUPS - 1, window, dil)

    @pl.when(g == N_GROUPS - 1)
    def _():
        y = (o_acc[...] / den_acc[...]) * _silu(z_ref[0])
        y_ref[...] = y.astype(y_ref.dtype)


def _attn_prompt(u3, *, batch, seq, h_g, q_off, k_off, v_off, z_off):
    for window, _ in DIL_GROUPS:
        assert seq % window == 0
    kern = functools.partial(_attn_prompt_kernel, seq=seq,
                             scale=float(LANES) ** -0.5 * 1.4426950408889634)

    def in_spec(off):
        return pl.BlockSpec((1, seq, LANES),
                            lambda b, h, g: (off + (N_GROUPS - 1 - g) * h_g + h, b, 0))

    return pl.pallas_call(
        kern,
        out_shape=jax.ShapeDtypeStruct((batch * seq, h_g * LANES), BF16),
        grid=(batch, h_g, N_GROUPS),
        in_specs=[in_spec(q_off), in_spec(k_off), in_spec(v_off),
                  pl.BlockSpec((1, seq, LANES), lambda b, h, g: (z_off + h, b, 0))],
        out_specs=pl.BlockSpec((seq, LANES), lambda b, h, g: (b, h)),
        scratch_shapes=[pltpu.VMEM((seq, LANES), F32)] * 3,
        compiler_params=_cparams(("parallel", "parallel", "arbitrary")),
        name="attn_prompt",
    )(u3, u3, u3, u3)


def _attn_sample_kernel(uq_ref, c0_ref, c1_ref, c2_ref, y_ref, *, td, h_g):
    def element(e, carry):
        _attn_sample_element(uq_ref, (c0_ref, c1_ref, c2_ref), y_ref, e, td=td, h_g=h_g)
        return carry

    lax.fori_loop(0, uq_ref.shape[0] // td, element, 0)


def _attn_sample_element(uq_ref, c_refs, y_ref, e, *, td, h_g):
    ro = e * td
    qscale = float(LANES) ** -0.5 * 1.4426950408889634
    neg_inf = -jnp.inf
    contract_last = (((1,), (1,)), ((), ()))
    q_part, k_part, v_part, z_part = 0, N_GROUPS, 2 * N_GROUPS, 3 * N_GROUPS
    wk = DIL_GROUPS[0][0] // DIL_GROUPS[0][1]
    nl = N_GROUPS * h_g
    shape3 = (wk, h_g, nl)
    c_row = lax.broadcasted_iota(jnp.int32, shape3, 0)
    key_head = lax.broadcasted_iota(jnp.int32, shape3, 1)
    lane = lax.broadcasted_iota(jnp.int32, shape3, 2)
    head_match = lane % h_g == key_head
    e_row = lax.broadcasted_iota(jnp.int32, (nl, LANES), 0)
    expand = [jnp.where(e_row // h_g == g, 1.0, 0.0).astype(BF16) for g in range(N_GROUPS)]
    zeros_q = jnp.zeros((h_g, LANES), F32)

    for i in range(td):
        r = ro + i
        valid = head_match
        k_parts, q_rows, v_cs, s_new, v_new = [], [], [], [], []
        for g, (window, dil) in enumerate(DIL_GROUPS):
            assert window // dil == wk
            c_ref = c_refs[g]
            q_t = uq_ref[r, q_part + g]
            if dil == 1:
                k_c, v_c = c_ref[e, :, 0], c_ref[e, :, 1]
                new_js = tuple(range(i + 1))
                valid = valid & ((lane // h_g != g) | (c_row >= i))
            else:
                k_c, v_c = c_ref[e, :, i, 0], c_ref[e, :, i, 1]
                new_js = (i,)
            k_parts.append(k_c.reshape(wk * h_g, LANES).astype(BF16))
            q_rows.append(jnp.concatenate(
                [q_t * qscale if gg == g else zeros_q for gg in range(N_GROUPS)], axis=1))
            v_cs.append(v_c)
            for jn in new_js:
                s_new.append(jnp.sum(uq_ref[ro + jn, k_part + g] * q_t, axis=-1, keepdims=True) * qscale)
                v_new.append(uq_ref[ro + jn, v_part + g])
        s = lax.dot_general(jnp.concatenate(k_parts, axis=1),
                            jnp.concatenate(q_rows, axis=0).astype(BF16),
                            contract_last, preferred_element_type=F32)
        s = jnp.where(valid, s.reshape(shape3), neg_inf)
        m = jnp.max(jnp.max(s, axis=0), axis=-1, keepdims=True)
        for sn in s_new:
            m = jnp.maximum(m, sn)
        p = jnp.exp2(s - m[None])
        den = jnp.sum(jnp.sum(p, axis=0), axis=-1, keepdims=True)
        p_bf = p.astype(BF16).reshape(wk * h_g, nl)
        y = jnp.zeros((h_g, LANES), F32)
        for g in range(N_GROUPS):
            pb = jnp.dot(p_bf, expand[g], preferred_element_type=F32)
            y = y + jnp.sum(pb.reshape(wk, h_g, LANES) * v_cs[g], axis=0)
        for sn, vn in zip(s_new, v_new):
            pn = jnp.exp2(sn - m)
            den = den + pn
            y = y + pn * vn
        y = (y / den) * _silu(uq_ref[r, z_part])
        y_ref[pl.ds(r, 1), :] = jnp.concatenate([y[h:h + 1, :] for h in range(h_g)], axis=1)


def _attn_sample(uq, caches, *, dec_batch, td, h_g):
    rows_blk = 8
    assert rows_blk % td == 0 and (dec_batch * td) % rows_blk == 0
    per_blk = rows_blk // td
    parts = uq.shape[1]
    views, c_specs = [], []
    for (window, dil), c in zip(DIL_GROUPS, caches):
        assert c.shape[1] == window, "window buffers must hold a full window"
        wk = window // dil
        if dil == 1:
            views.append(c)
            c_specs.append(pl.BlockSpec((per_blk, wk, 2, h_g, LANES), lambda s: (s, 0, 0, 0, 0)))
        else:
            assert td <= dil
            views.append(c.reshape(dec_batch, wk, dil, 2, h_g, LANES))
            c_specs.append(pl.BlockSpec((per_blk, wk, td, 2, h_g, LANES), lambda s: (s, 0, 0, 0, 0, 0)))
    kern = functools.partial(_attn_sample_kernel, td=td, h_g=h_g)
    return pl.pallas_call(
        kern,
        out_shape=jax.ShapeDtypeStruct((dec_batch * td, h_g * LANES), F32),
        grid=(dec_batch // per_blk,),
        in_specs=[pl.BlockSpec((rows_blk, parts, h_g, LANES), lambda s: (s, 0, 0, 0))] + c_specs,
        out_specs=pl.BlockSpec((rows_blk, h_g * LANES), lambda s: (s, 0)),
        compiler_params=_cparams(("arbitrary",)),
        name="attn_sample",
    )(uq, *views)


def _merge_kernel(ya_ref, yb_ref, wa_ref, wb_ref, ga_ref, gb_ref, o_ref, *, nc, blocked_gates):
    pa = jnp.dot(ya_ref[...].astype(BF16), wa_ref[...], preferred_element_type=F32)
    pb = jnp.dot(yb_ref[...].astype(BF16), wb_ref[...], preferred_element_type=F32)
    for c in range(nc):
        ls = slice(c * LANES, (c + 1) * LANES)
        ga = ga_ref[c] if blocked_gates else ga_ref[:, ls]
        gb = gb_ref[c] if blocked_gates else gb_ref[:, ls]
        o_ref[:, ls] = (_sigmoid(ga) * pa[:, ls] + _sigmoid(gb) * pb[:, ls]).astype(o_ref.dtype)


def _merge(ya, yb, wa_bf, wb_bf, gates, *, ga_off, gb_off, blocked_gates):
    m = ya.shape[0]
    d = wa_bf.shape[1]
    tm = ROW_TILE
    tn = _largest_divisor(d, (1024, 512, 256, 128))
    nc = tn // LANES
    assert (ga_off * LANES) % tn == 0 and (gb_off * LANES) % tn == 0
    ga_blk, gb_blk = ga_off * LANES // tn, gb_off * LANES // tn
    if blocked_gates:
        ga_spec = pl.BlockSpec((nc, tm, LANES), lambda j, i: (ga_blk + j, i, 0))
        gb_spec = pl.BlockSpec((nc, tm, LANES), lambda j, i: (gb_blk + j, i, 0))
    else:
        ga_spec = pl.BlockSpec((tm, tn), lambda j, i: (i, ga_blk + j))
        gb_spec = pl.BlockSpec((tm, tn), lambda j, i: (i, gb_blk + j))
    kern = functools.partial(_merge_kernel, nc=nc, blocked_gates=blocked_gates)
    return pl.pallas_call(
        kern,
        out_shape=jax.ShapeDtypeStruct((m, d), BF16),
        grid=(d // tn, m // tm),
        in_specs=[pl.BlockSpec((tm, ya.shape[1]), lambda j, i: (i, 0)),
                  pl.BlockSpec((tm, yb.shape[1]), lambda j, i: (i, 0)),
                  pl.BlockSpec((wa_bf.shape[0], tn), lambda j, i: (0, j)),
                  pl.BlockSpec((wb_bf.shape[0], tn), lambda j, i: (0, j)),
                  ga_spec, gb_spec],
        out_specs=pl.BlockSpec((tm, tn), lambda j, i: (i, j)),
        compiler_params=_cparams(("parallel", "parallel")),
        name="merge_blocked" if blocked_gates else "merge_rows",
    )(ya, yb, wa_bf, wb_bf, gates, gates)


def _outproj_kernel(mg_ref, w_ref, x_ref, fw_ref, y_ref, *, tn):
    d = y_ref.shape[1]
    mg = mg_ref[...]
    for c in range(d // tn):
        cols = slice(c * tn, (c + 1) * tn)
        y_ref[:, cols] = x_ref[:, cols] + jnp.dot(mg, w_ref[:, cols], preferred_element_type=F32)

    rc = 64
    fw = fw_ref[...]

    def norm_rows(c, carry):
        rows = pl.ds(pl.multiple_of(c * rc, rc), rc)
        full = y_ref[rows, :]
        ms = jnp.mean(full * full, axis=-1, keepdims=True)
        y_ref[rows, :] = full * lax.rsqrt(ms + RMS_EPS) * fw
        return carry

    lax.fori_loop(0, y_ref.shape[0] // rc, norm_rows, 0)


def _outproj(merged, w_bf, x, fw):
    m, d = x.shape
    tm = ROW_TILE // 2
    tn = _largest_divisor(d, (1024, 512, 256, 128))
    kern = functools.partial(_outproj_kernel, tn=tn)
    return pl.pallas_call(
        kern,
        out_shape=jax.ShapeDtypeStruct((m, d), F32),
        grid=(m // tm,),
        in_specs=[pl.BlockSpec((tm, d), lambda i: (i, 0)),
                  pl.BlockSpec((d, d), lambda i: (0, 0), pipeline_mode=pl.Buffered(1)),
                  pl.BlockSpec((tm, d), lambda i: (i, 0)),
                  pl.BlockSpec((1, d), lambda i: (0, 0))],
        out_specs=pl.BlockSpec((tm, d), lambda i: (i, 0)),
        compiler_params=_cparams(("arbitrary",)),
        name="outproj_norm",
    )(merged, w_bf, x, fw.reshape(1, d))


def kernel(x_prompt, x_sample, cache_kv_w128, cache_kv_w512, cache_kv_w2048, state_hgrn,
           norm_w, w_in, lb_logits, hgrn_norm_w, w_proj_a, w_proj_b, w_out, final_norm_w):
    depth = norm_w.shape[0]
    assert depth == 1, "single-layer step"
    layer = 0
    batch, seq, d = x_prompt.shape
    dec_batch, td, _ = x_sample.shape
    w_a = w_proj_a.shape[1]
    w_bo = w_proj_b.shape[1]
    n_heads = w_a // LANES
    h_g = w_bo // LANES
    n_b = N_GROUPS * h_g
    n_d = d // LANES
    qb_off = 4 * n_heads
    kb_off, vb_off = qb_off + n_b, qb_off + 2 * n_b
    zb_off = qb_off + 3 * n_b
    ga_off = zb_off + h_g
    gb_off = ga_off + n_d
    assert (gb_off + n_d) * LANES == w_in.shape[2]

    xp = x_prompt.reshape(batch * seq, d)
    xs = x_sample.reshape(dec_batch * td, d)
    caches = (cache_kv_w128[layer], cache_kv_w512[layer], cache_kv_w2048[layer])

    wa_bf = _cast_bf16(w_proj_a[layer])
    wb_bf = _cast_bf16(w_proj_b[layer])
    wo_bf = _cast_bf16(w_out[layer])

    h_p = _norm_cast(xp, norm_w[layer])
    h_s = _norm_cast(xs, norm_w[layer])
    u3, us, uq, *kv_p = _inproj(h_p, h_s, w_in[layer], batch=batch, seq=seq, h_g=h_g,
                                attn_off=qb_off, attn_parts=3 * N_GROUPS + 1)
    kv_p = [kv.reshape(1, batch, kv.shape[0] // batch, 2, h_g, LANES) for kv in kv_p]

    nw = hgrn_norm_w[layer].reshape(1, LANES)
    ya_p, s_p = _hgrn_prompt(u3, lb_logits, nw, batch=batch, seq=seq, n_heads=n_heads, layer=layer)
    ya_s, s_s = _hgrn_sample(us, state_hgrn[layer], lb_logits, nw,
                             dec_batch=dec_batch, td=td, n_heads=n_heads, layer=layer)

    yb_p = _attn_prompt(u3, batch=batch, seq=seq, h_g=h_g,
                        q_off=qb_off, k_off=kb_off, v_off=vb_off, z_off=zb_off)
    yb_s = _attn_sample(uq, caches, dec_batch=dec_batch, td=td, h_g=h_g)

    mg_p = _merge(ya_p, yb_p, wa_bf, wb_bf, u3, ga_off=ga_off, gb_off=gb_off, blocked_gates=True)
    mg_s = _merge(ya_s, yb_s, wa_bf, wb_bf, us, ga_off=ga_off, gb_off=gb_off, blocked_gates=False)
    y_p = _outproj(mg_p, wo_bf, xp, final_norm_w)
    y_s = _outproj(mg_s, wo_bf, xs, final_norm_w)

    kv_s = [jnp.stack([uq[:, N_GROUPS + g], uq[:, 2 * N_GROUPS + g]], axis=1)
            .reshape(1, dec_batch, td, 2, h_g, LANES) for g in range(N_GROUPS)]

    return (y_p.reshape(batch, seq, d), y_s.reshape(dec_batch, td, d),
            kv_p[0], kv_p[1], kv_p[2], s_p[None],
            kv_s[0], kv_s[1], kv_s[2], s_s[None])
```

```python
import functools

import jax
import jax.numpy as jnp
from jax import lax
from jax.experimental import pallas as pl
from jax.experimental.pallas import tpu as pltpu

F32 = jnp.float32
BF16 = jnp.bfloat16

LANES = 128
RMS_EPS = 1e-6
LOG2_E = 1.4426950408889634
HGRN_CHUNK = 64
DIL_GROUPS = ((128, 1), (512, 4), (2048, 16))
N_GROUPS = len(DIL_GROUPS)
ROW_TILE = 512
VMEM_LIMIT = 56 * 1024 * 1024


def _cparams(semantics, vmem=VMEM_LIMIT):
    return pltpu.CompilerParams(dimension_semantics=semantics, vmem_limit_bytes=vmem)


def _sigmoid(x):
    return 1.0 / (1.0 + jnp.exp(-x))


def _silu(x):
    return x * _sigmoid(x)


def _largest_divisor(n, candidates):
    for c in candidates:
        if n % c == 0:
            return c
    raise ValueError(f"no tile in {candidates} divides {n}")


def _norm_cast_kernel(x_ref, w_ref, o_ref):
    x = x_ref[...]
    ms = jnp.mean(x * x, axis=-1, keepdims=True)
    o_ref[...] = (x * lax.rsqrt(ms + RMS_EPS) * w_ref[...]).astype(o_ref.dtype)


def _norm_cast(x, w):
    m, d = x.shape
    tm = _largest_divisor(m, (ROW_TILE, 256, 128))
    return pl.pallas_call(
        _norm_cast_kernel,
        out_shape=jax.ShapeDtypeStruct((m, d), BF16),
        grid=(m // tm,),
        in_specs=[pl.BlockSpec((tm, d), lambda i: (i, 0)),
                  pl.BlockSpec((1, d), lambda i: (0, 0))],
        out_specs=pl.BlockSpec((tm, d), lambda i: (i, 0)),
        compiler_params=_cparams(("parallel",)),
        name="norm_cast",
    )(x, w.reshape(1, d))


def _cast_kernel(x_ref, o_ref):
    o_ref[...] = x_ref[...].astype(o_ref.dtype)


def _cast_bf16(w):
    r, c = w.shape
    tr = _largest_divisor(r, (512, 256, 128))
    tc = _largest_divisor(c, (2048, 1024, 512, 256, 128))
    return pl.pallas_call(
        _cast_kernel,
        out_shape=jax.ShapeDtypeStruct((r, c), BF16),
        grid=(r // tr, c // tc),
        in_specs=[pl.BlockSpec((tr, tc), lambda i, j: (i, j))],
        out_specs=pl.BlockSpec((tr, tc), lambda i, j: (i, j)),
        compiler_params=_cparams(("parallel", "parallel")),
        name="cast_bf16",
    )(w)


def _inproj_kernel(hp_ref, hs_ref, w_hbm, u3_ref, us_ref, uq_ref, *rest,
                   n_p, nc, h_g, j_lo, j_hi, kv_plan, tiles_per_seq, n_chunks):
    kv_refs = rest[:len(kv_plan)]
    wbf_ref, stage_ref, sem = rest[len(kv_plan):]
    j = pl.program_id(0)
    i = pl.program_id(1)
    _, kc, tn = stage_ref.shape
    cur = j % 2
    nxt = 1 - cur

    def chunk_copy(tile, c, slot):
        return pltpu.make_async_copy(
            w_hbm.at[pl.ds(pl.multiple_of(c * kc, kc), kc), pl.ds(pl.multiple_of(tile * tn, tn), tn)],
            stage_ref.at[slot], sem.at[slot])

    def cast_chunk(c, slot, buf):
        wbf_ref[buf, pl.ds(pl.multiple_of(c * kc, kc), kc), :] = stage_ref[slot].astype(BF16)

    @pl.when((j == 0) & (i == 0))
    def _():
        def first_tile(c, carry):
            cp = chunk_copy(0, c, c % 2)
            cp.start()
            cp.wait()
            cast_chunk(c, c % 2, 0)
            return carry
        lax.fori_loop(0, n_chunks, first_tile, 0)

    has_next = j + 1 < pl.num_programs(0)

    @pl.when(has_next & (i >= 1) & (i <= n_chunks))
    def _():
        chunk_copy(j + 1, i - 1, (i + 1) % 2).wait()

    @pl.when(has_next & (i < n_chunks))
    def _():
        chunk_copy(j + 1, i, i % 2).start()

    def cast_arrived_chunk():
        cast_chunk(jnp.clip(i - 1, 0, n_chunks - 1), (i + 1) % 2, nxt)

    @pl.when(i < n_p)
    def _():
        cast_arrived_chunk()
        acc = jnp.dot(hp_ref[...], wbf_ref[cur], preferred_element_type=F32)
        for c in range(nc):
            u3_ref[c] = acc[:, c * LANES:(c + 1) * LANES]

        tile_in_seq = i % tiles_per_seq
        for kv_ref, (jk, ck, jv, cv, n_keep_tiles) in zip(kv_refs, kv_plan):
            keep_rows = kv_ref.shape[0]
            for jw, c0 in ((jk, ck), (jv, cv)):
                @pl.when((j == jw) & (tile_in_seq >= tiles_per_seq - n_keep_tiles))
                def _(kv_ref=kv_ref, c0=c0, keep_rows=keep_rows):
                    r0 = acc.shape[0] - keep_rows
                    flat = kv_ref.reshape(keep_rows * h_g, LANES)
                    for h in range(h_g):
                        flat[pl.ds(h, keep_rows, stride=h_g), :] = (
                            acc[r0:, (c0 + h) * LANES:(c0 + h + 1) * LANES])

    @pl.when(i >= n_p)
    def _():
        cast_arrived_chunk()
        acc = jnp.dot(hs_ref[...], wbf_ref[cur], preferred_element_type=F32)
        us_ref[...] = acc

        @pl.when((j >= j_lo) & (j < j_hi))
        def _():
            rows, parts = uq_ref.shape[0], uq_ref.shape[1]
            flat = uq_ref.reshape(rows * parts * h_g, LANES)
            for c in range(nc):
                flat[pl.ds(c, rows, stride=parts * h_g), :] = acc[:, c * LANES:(c + 1) * LANES]


def _inproj(h_p, h_s, w_f32, *, batch, seq, h_g, attn_off, attn_parts):
    mp, d = h_p.shape
    ms = h_s.shape[0]
    n = w_f32.shape[1]
    tm = ROW_TILE
    tn = _largest_divisor(n, (1024, 512, 256, 128))
    n_p, n_s = mp // tm, ms // tm
    n_chunks = n_p + n_s - 1
    assert d % n_chunks == 0 and (d // n_chunks) % 16 == 0
    kc = d // n_chunks
    nc = tn // LANES
    assert nc % h_g == 0 and (attn_off * LANES) % tn == 0 and (attn_parts * h_g) % nc == 0
    ppt = nc // h_g
    j_lo = attn_off * LANES // tn
    n_aj = attn_parts // ppt
    assert seq % tm == 0
    tiles_per_seq = seq // tm

    kv_plan, kv_shapes, kv_specs = [], [], []
    for g, (window, _) in enumerate(DIL_GROUPS):
        keep = min(window, seq)
        blk_rows = min(keep, tm)
        assert keep % blk_rows == 0 and seq % keep == 0
        nkt = keep // blk_rows
        k_part, v_part = N_GROUPS + g, 2 * N_GROUPS + g
        jk, ck = j_lo + k_part // ppt, (k_part % ppt) * h_g
        jv, cv = j_lo + v_part // ppt, (v_part % ppt) * h_g
        assert jk < jv
        kv_plan.append((jk, ck, jv, cv, nkt))
        kv_shapes.append(jax.ShapeDtypeStruct((batch * keep, 2, h_g, LANES), F32))
        last = batch * nkt - 1

        def kv_index(j, i, jk=jk, jv=jv, nkt=nkt, last=last):
            ip = jnp.minimum(i, n_p - 1)
            moving = (ip // tiles_per_seq) * nkt + jnp.clip(ip % tiles_per_seq - (tiles_per_seq - nkt), 0, nkt - 1)
            rb = jnp.where(j < jk, 0, jnp.where((j == jk) | (j == jv), moving, last))
            return (rb, (j >= jv).astype(jnp.int32), 0, 0)

        kv_specs.append(pl.BlockSpec((blk_rows, 1, h_g, LANES), kv_index))

    kern = functools.partial(_inproj_kernel, n_p=n_p, nc=nc, h_g=h_g, j_lo=j_lo, j_hi=j_lo + n_aj,
                             kv_plan=tuple(kv_plan), tiles_per_seq=tiles_per_seq, n_chunks=n_chunks)
    return pl.pallas_call(
        kern,
        out_shape=(jax.ShapeDtypeStruct((n // LANES, mp, LANES), F32),
                   jax.ShapeDtypeStruct((ms, n), F32),
                   jax.ShapeDtypeStruct((ms, attn_parts, h_g, LANES), F32),
                   *kv_shapes),
        grid=(n // tn, n_p + n_s),
        in_specs=[
            pl.BlockSpec((tm, d), lambda j, i: (jnp.minimum(i, n_p - 1), 0)),
            pl.BlockSpec((tm, d), lambda j, i: (jnp.clip(i - n_p, 0, n_s - 1), 0),
                         pipeline_mode=pl.Buffered(1) if n_s == 1 else None),
            pl.BlockSpec(memory_space=pl.ANY),
        ],
        out_specs=(
            pl.BlockSpec((nc, tm, LANES), lambda j, i: (j, jnp.minimum(i, n_p - 1), 0)),
            pl.BlockSpec((tm, tn), lambda j, i: (jnp.clip(i - n_p, 0, n_s - 1), j)),
            pl.BlockSpec((tm, ppt, h_g, LANES),
                         lambda j, i: (jnp.clip(i - n_p, 0, n_s - 1), jnp.clip(j - j_lo, 0, n_aj - 1), 0, 0)),
            *kv_specs,
        ),
        scratch_shapes=[pltpu.VMEM((2, d, tn), BF16),
                        pltpu.VMEM((2, kc, tn), F32),
                        pltpu.SemaphoreType.DMA((2,))],
        compiler_params=_cparams(("arbitrary", "arbitrary")),
        name="inproj",
    )(h_p, h_s, w_f32)


def _lower_bound(lb_logits, layer):
    mx = jnp.max(lb_logits, axis=0, keepdims=True)
    e = jnp.exp(lb_logits - mx)
    den = jnp.sum(e, axis=0, keepdims=True)
    num = e[0:1]
    for r in range(1, layer + 1):
        num = num + e[r:r + 1]
    return num / den


def _split3_bf16(x):
    hi = x.astype(BF16)
    r1 = x - hi.astype(F32)
    mid = r1.astype(BF16)
    lo = (r1 - mid.astype(F32)).astype(BF16)
    return hi, mid, lo


def _hgrn_prompt_kernel(q_ref, f_ref, i_ref, z_ref, lb_ref, nw_ref, y_ref, sfin_ref, st_ref,
                        *, hb, tb, chunk, cpb, layer):
    t = pl.program_id(2)

    @pl.when(t == 0)
    def _():
        st_ref[...] = jnp.zeros_like(st_ref)

    lb_all = _lower_bound(lb_ref[...], layer)
    nw = nw_ref[...]
    rb = cpb * chunk
    row = lax.broadcasted_iota(jnp.int32, (rb, rb), 0)
    col = lax.broadcasted_iota(jnp.int32, (rb, rb), 1)
    causal = (row >= col) & (row // chunk == col // chunk)
    tril = jnp.where(causal, 1.0, 0.0).astype(BF16)
    dk = q_ref.shape[-1]
    qscale = float(dk) ** -0.5
    contract_last = (((1,), (1,)), ((), ()))
    contract_first = (((0,), (0,)), ((), ()))

    def block_body(blk, carry):
        r0 = pl.multiple_of(blk * rb, rb)
        for hl in range(hb):
            lb = lb_all[:, hl * LANES:(hl + 1) * LANES]
            qa = q_ref[hl, pl.ds(r0, rb), :]
            fa = f_ref[hl, pl.ds(r0, rb), :]
            v = i_ref[hl, pl.ds(r0, rb), :]
            za = z_ref[hl, pl.ds(r0, rb), :]
            q = _silu(qa) * qscale
            f = lb + (1.0 - lb) * _sigmoid(fa)
            log_f = jnp.log(f)
            k = 1.0 - f
            hi, mid, lo = _split3_bf16(log_f)
            parts = jnp.dot(tril, jnp.concatenate([hi, mid, lo], axis=1),
                            preferred_element_type=F32)
            g = (parts[:, 0:LANES] + parts[:, LANES:2 * LANES]) + parts[:, 2 * LANES:3 * LANES]
            g_last = [g[(c + 1) * chunk - 1:(c + 1) * chunk, :] for c in range(cpb)]
            g_last_rows = jnp.concatenate(
                [jnp.broadcast_to(gl, (chunk, LANES)) for gl in g_last], axis=0)
            q_dec = (q * jnp.exp(g)).astype(BF16)
            k_rel = (k * jnp.exp(-g)).astype(BF16)
            k_end = (k * jnp.exp(g_last_rows - g)).astype(BF16)
            vb = v.astype(BF16)
            a = lax.dot_general(q_dec, k_rel, contract_last, preferred_element_type=F32)
            a = jnp.where(causal, a, 0.0)
            o_intra = jnp.dot(a.astype(BF16), vb, preferred_element_type=F32)
            st = st_ref[hl]
            o_inter = []
            for c in range(cpb):
                rows = slice(c * chunk, (c + 1) * chunk)
                o_inter.append(lax.dot_general(q_dec[rows], st.astype(BF16), contract_last,
                                               preferred_element_type=F32))
                ds_t = lax.dot_general(vb[rows], k_end[rows], contract_first,
                                       preferred_element_type=F32)
                st = jnp.exp(g_last[c]) * st + ds_t
            st_ref[hl] = st
            o = o_intra + jnp.concatenate(o_inter, axis=0)
            ms = jnp.mean(o * o, axis=-1, keepdims=True)
            y = (o * lax.rsqrt(ms + RMS_EPS) * nw) * _silu(za)
            y_ref[pl.ds(r0, rb), hl * LANES:(hl + 1) * LANES] = y.astype(y_ref.dtype)
        return carry

    lax.fori_loop(0, tb // rb, block_body, 0)

    @pl.when(t == pl.num_programs(2) - 1)
    def _():
        for hl in range(hb):
            sfin_ref[0, hl] = st_ref[hl].T


def _hgrn_prompt(u3, lb_logits, nw, *, batch, seq, n_heads, layer):
    hb = _largest_divisor(n_heads, (8, 4, 2, 1))
    tb = _largest_divisor(seq, (1024, 512, 256, 128, 64))
    chunk = HGRN_CHUNK
    assert tb % chunk == 0
    nt = seq // tb
    nhb = n_heads // hb
    cpb = _largest_divisor(tb // chunk, (4, 2, 1))
    kern = functools.partial(_hgrn_prompt_kernel, hb=hb, tb=tb, chunk=chunk, cpb=cpb, layer=layer)

    def in_spec(part):
        return pl.BlockSpec((hb, tb, LANES), lambda b, h, t: (part * nhb + h, b * nt + t, 0))

    return pl.pallas_call(
        kern,
        out_shape=(jax.ShapeDtypeStruct((batch * seq, n_heads * LANES), BF16),
                   jax.ShapeDtypeStruct((batch, n_heads, LANES, LANES), F32)),
        grid=(batch, nhb, nt),
        in_specs=[in_spec(0), in_spec(1), in_spec(2), in_spec(3),
                  pl.BlockSpec((lb_logits.shape[0], hb * LANES), lambda b, h, t: (0, h)),
                  pl.BlockSpec((1, LANES), lambda b, h, t: (0, 0))],
        out_specs=(pl.BlockSpec((tb, hb * LANES), lambda b, h, t: (b * nt + t, h)),
                   pl.BlockSpec((1, hb, LANES, LANES), lambda b, h, t: (b, h, 0, 0))),
        scratch_shapes=[pltpu.VMEM((hb, LANES, LANES), F32)],
        compiler_params=_cparams(("parallel", "parallel", "arbitrary")),
        name="hgrn_prompt",
    )(u3, u3, u3, u3, lb_logits, nw)


def _hgrn_sample_kernel(q_ref, f_ref, i_ref, z_ref, lb_ref, nw_ref, s_ref, y_ref, so_ref, o_scr,
                        *, bb, td, layer):
    lb = _lower_bound(lb_ref[...], layer)
    dk = q_ref.shape[-1]
    q = _silu(q_ref[...]) * (float(dk) ** -0.5)
    f = lb + (1.0 - lb) * _sigmoid(f_ref[...])
    k = 1.0 - f
    v = i_ref[...]
    rows = bb * td
    sub = 8
    contract_last = (((1,), (1,)), ((), ()))
    contract_first = (((0,), (0,)), ((), ()))
    t_idx = lax.broadcasted_iota(jnp.int32, (rows, LANES), 0) % td
    p_inc = f
    s = 1
    while s < td:
        p_inc = p_inc * jnp.where(t_idx >= s, pltpu.roll(p_inc, s, axis=0), 1.0)
        s *= 2
    e_suf = jnp.where(t_idx + 1 < td, pltpu.roll(f, rows - 1, axis=0), 1.0)
    s = 1
    while s < td:
        e_suf = e_suf * jnp.where(t_idx + s < td, pltpu.roll(e_suf, rows - s, axis=0), 1.0)
        s *= 2
    q_dec = q * p_inc
    k_end = k * e_suf
    a = lax.dot_general(q_dec.astype(BF16), (k / p_inc).astype(BF16), contract_last,
                        preferred_element_type=F32)
    ri = lax.broadcasted_iota(jnp.int32, (rows, rows), 0)
    ci = lax.broadcasted_iota(jnp.int32, (rows, rows), 1)
    a = jnp.where((ri // td == ci // td) & (ci <= ri), a, 0.0)
    o_intra = jnp.dot(a.astype(BF16), v.astype(BF16), preferred_element_type=F32)
    dec_t = p_inc.T
    row8 = lax.broadcasted_iota(jnp.int32, (sub, LANES), 0)
    for grp in range(rows // sub):
        sl = slice(grp * sub, (grp + 1) * sub)
        qd8 = q_dec[sl].astype(BF16)
        o8 = jnp.zeros((sub, LANES), F32)
        for u in range(sub // td):
            bl = grp * (sub // td) + u
            mine = (row8 >= u * td) & (row8 < (u + 1) * td)
            s0 = s_ref[bl, 0]
            o8 = jnp.where(mine, jnp.dot(qd8, s0.astype(BF16), preferred_element_type=F32), o8)
            ds = lax.dot_general(jnp.where(mine, k_end[sl], 0.0).astype(BF16),
                                 jnp.where(mine, v[sl], 0.0).astype(BF16), contract_first,
                                 preferred_element_type=F32)
            j = bl * td + td - 1
            so_ref[bl, 0] = dec_t[:, j:j + 1] * s0 + ds
        o_scr[sl, :] = o8
    o = o_intra + o_scr[...]
    ms = jnp.mean(o * o, axis=-1, keepdims=True)
    y = (o * lax.rsqrt(ms + RMS_EPS) * nw_ref[...]) * _silu(z_ref[...])
    y_ref[...] = y.astype(y_ref.dtype)


def _hgrn_sample(us, state, lb_logits, nw, *, dec_batch, td, n_heads, layer):
    rows = LANES
    assert rows % td == 0 and 8 % td == 0
    bb = rows // td
    assert dec_batch % bb == 0
    kern = functools.partial(_hgrn_sample_kernel, bb=bb, td=td, layer=layer)

    def in_spec(part):
        return pl.BlockSpec((rows, LANES), lambda bi, h: (bi, part * n_heads + h))

    return pl.pallas_call(
        kern,
        out_shape=(jax.ShapeDtypeStruct((dec_batch * td, n_heads * LANES), BF16),
                   jax.ShapeDtypeStruct(state.shape, F32)),
        grid=(dec_batch // bb, n_heads),
        in_specs=[in_spec(0), in_spec(1), in_spec(2), in_spec(3),
                  pl.BlockSpec((lb_logits.shape[0], LANES), lambda bi, h: (0, h)),
                  pl.BlockSpec((1, LANES), lambda bi, h: (0, 0)),
                  pl.BlockSpec((bb, 1, LANES, LANES), lambda bi, h: (bi, h, 0, 0))],
        out_specs=(pl.BlockSpec((rows, LANES), lambda bi, h: (bi, h)),
                   pl.BlockSpec((bb, 1, LANES, LANES), lambda bi, h: (bi, h, 0, 0))),
        scratch_shapes=[pltpu.VMEM((rows, LANES), F32)],
        compiler_params=_cparams(("parallel", "parallel")),
        name="hgrn_sample",
    )(us, us, us, us, lb_logits, nw, state)


def _attn_prompt_kernel(q_ref, k_ref, v_ref, z_ref, y_ref, m_acc, den_acc, o_acc, *, seq, scale):
    g = pl.program_id(2)
    neg_inf = -jnp.inf
    contract_last = (((1,), (1,)), ((), ()))

    def run_group(first, window, dil):
        wk = window // dil
        nb = seq // window
        ii = lax.broadcasted_iota(jnp.int32, (wk, wk), 0)
        jj = lax.broadcasted_iota(jnp.int32, (wk, wk), 1)
        mask_prev = jj >= ii
        mask_own = jj <= ii
        ones = jnp.ones((wk, LANES), BF16)

        def rows(start):
            return pl.ds(start, wk, stride=dil) if dil > 1 else pl.ds(start, wk)

        def block(idx, carry):
            n, r = idx // dil, idx % dil
            own = rows(n * window + r)
            prev = rows(jnp.maximum(n - 1, 0) * window + r)
            q = (q_ref[0, own, :] * scale).astype(BF16)
            k_own = k_ref[0, own, :].astype(BF16)
            k_prev = k_ref[0, prev, :].astype(BF16)
            v_own = jnp.concatenate([v_ref[0, own, :].astype(BF16), ones], axis=1)
            v_prev = jnp.concatenate([v_ref[0, prev, :].astype(BF16), ones], axis=1)
            s_own = lax.dot_general(q, k_own, contract_last, preferred_element_type=F32)
            s_prev = lax.dot_general(q, k_prev, contract_last, preferred_element_type=F32)
            no_prev = jnp.where(n > 0, 0.0, neg_inf)
            s_own = jnp.where(mask_own, s_own, neg_inf)
            s_prev = jnp.where(mask_prev, s_prev + no_prev, neg_inf)
            m = jnp.max(jnp.maximum(s_own, s_prev), axis=-1, keepdims=True)
            p_own = jnp.exp2(s_own - m).astype(BF16)
            p_prev = jnp.exp2(s_prev - m).astype(BF16)
            acc = (jnp.dot(p_own, v_own, preferred_element_type=F32)
                   + jnp.dot(p_prev, v_prev, preferred_element_type=F32))
            o, den = acc[:, :LANES], acc[:, LANES:]
            if first:
                m_acc[own, :] = jnp.broadcast_to(m, (wk, LANES))
                den_acc[own, :] = den
                o_acc[own, :] = o
            else:
                m_old = m_acc[own, :]
                m_new = jnp.maximum(m_old, m)
                a_old = jnp.exp2(m_old - m_new)
                a_new = jnp.exp2(m - m_new)
                m_acc[own, :] = m_new
                den_acc[own, :] = den_acc[own, :] * a_old + den * a_new
                o_acc[own, :] = o_acc[own, :] * a_old + o * a_new
            return carry

        lax.fori_loop(0, nb * dil, block, 0, unroll=8)

    for gi, (window, dil) in enumerate(DIL_GROUPS):
        @pl.when(g == N_GROUPS - 1 - gi)
        def _(gi=gi, window=window, dil=dil):
            run_group(gi == N_GROUPS - 1, window, dil)

    @pl.when(g == N_GROUPS - 1)
    def _():
        y = (o_acc[...] / den_acc[...]) * _silu(z_ref[0])
        y_ref[...] = y.astype(y_ref.dtype)


def _attn_prompt(u3, *, batch, seq, h_g, q_off, k_off, v_off, z_off):
    for window, _ in DIL_GROUPS:
        assert seq % window == 0
    kern = functools.partial(_attn_prompt_kernel, seq=seq,
                             scale=float(LANES) ** -0.5 * LOG2_E)

    def in_spec(off):
        return pl.BlockSpec((1, seq, LANES),
                            lambda b, h, g: (off + (N_GROUPS - 1 - g) * h_g + h, b, 0))

    return pl.pallas_call(
        kern,
        out_shape=jax.ShapeDtypeStruct((batch * seq, h_g * LANES), BF16),
        grid=(batch, h_g, N_GROUPS),
        in_specs=[in_spec(q_off), in_spec(k_off), in_spec(v_off),
                  pl.BlockSpec((1, seq, LANES), lambda b, h, g: (z_off + h, b, 0))],
        out_specs=pl.BlockSpec((seq, LANES), lambda b, h, g: (b, h)),
        scratch_shapes=[pltpu.VMEM((seq, LANES), F32)] * 3,
        compiler_params=_cparams(("parallel", "parallel", "arbitrary")),
        name="attn_prompt",
    )(u3, u3, u3, u3)


def _attn_sample_kernel(uq_ref, c0_ref, c1_ref, c2_ref, y_ref, *, td, h_g):
    def element(e, carry):
        _attn_sample_element(uq_ref, (c0_ref, c1_ref, c2_ref), y_ref, e, td=td, h_g=h_g)
        return carry

    lax.fori_loop(0, uq_ref.shape[0] // td, element, 0)


def _attn_sample_element(uq_ref, c_refs, y_ref, e, *, td, h_g):
    ro = e * td
    qscale = float(LANES) ** -0.5 * LOG2_E
    neg_inf = -jnp.inf
    contract_last = (((1,), (1,)), ((), ()))
    q_part, k_part, v_part, z_part = 0, N_GROUPS, 2 * N_GROUPS, 3 * N_GROUPS
    wk = DIL_GROUPS[0][0] // DIL_GROUPS[0][1]
    nl = N_GROUPS * h_g
    shape3 = (wk, h_g, nl)
    c_row = lax.broadcasted_iota(jnp.int32, shape3, 0)
    key_head = lax.broadcasted_iota(jnp.int32, shape3, 1)
    lane = lax.broadcasted_iota(jnp.int32, shape3, 2)
    head_match = lane % h_g == key_head
    e_row = lax.broadcasted_iota(jnp.int32, (nl, LANES), 0)
    expand = [jnp.where(e_row // h_g == g, 1.0, 0.0).astype(BF16) for g in range(N_GROUPS)]
    zeros_q = jnp.zeros((h_g, LANES), F32)

    for i in range(td):
        r = ro + i
        valid = head_match
        k_parts, q_rows, v_cs, s_new, v_new = [], [], [], [], []
        for g, (window, dil) in enumerate(DIL_GROUPS):
            assert window // dil == wk
            c_ref = c_refs[g]
            q_t = uq_ref[r, q_part + g]
            if dil == 1:
                k_c, v_c = c_ref[e, :, 0], c_ref[e, :, 1]
                new_js = tuple(range(i + 1))
                valid = valid & ((lane // h_g != g) | (c_row >= i))
            else:
                k_c, v_c = c_ref[e, :, i, 0], c_ref[e, :, i, 1]
                new_js = (i,)
            k_parts.append(k_c.reshape(wk * h_g, LANES).astype(BF16))
            q_rows.append(jnp.concatenate(
                [q_t * qscale if gg == g else zeros_q for gg in range(N_GROUPS)], axis=1))
            v_cs.append(v_c)
            for jn in new_js:
                s_new.append(jnp.sum(uq_ref[ro + jn, k_part + g] * q_t, axis=-1, keepdims=True) * qscale)
                v_new.append(uq_ref[ro + jn, v_part + g])
        s = lax.dot_general(jnp.concatenate(k_parts, axis=1),
                            jnp.concatenate(q_rows, axis=0).astype(BF16),
                            contract_last, preferred_element_type=F32)
        s = jnp.where(valid, s.reshape(shape3), neg_inf)
        m = jnp.max(jnp.max(s, axis=0), axis=-1, keepdims=True)
        for sn in s_new:
            m = jnp.maximum(m, sn)
        p = jnp.exp2(s - m[None])
        den = jnp.sum(jnp.sum(p, axis=0), axis=-1, keepdims=True)
        p_bf = p.astype(BF16).reshape(wk * h_g, nl)
        y = jnp.zeros((h_g, LANES), F32)
        for g in range(N_GROUPS):
            pb = jnp.dot(p_bf, expand[g], preferred_element_type=F32)
            y = y + jnp.sum(pb.reshape(wk, h_g, LANES) * v_cs[g], axis=0)
        for sn, vn in zip(s_new, v_new):
            pn = jnp.exp2(sn - m)
            den = den + pn
            y = y + pn * vn
        y = (y / den) * _silu(uq_ref[r, z_part])
        y_ref[pl.ds(r, 1), :] = jnp.concatenate([y[h:h + 1, :] for h in range(h_g)], axis=1)


def _attn_sample(uq, caches, *, dec_batch, td, h_g):
    rows_blk = 8
    assert rows_blk % td == 0 and (dec_batch * td) % rows_blk == 0
    per_blk = rows_blk // td
    parts = uq.shape[1]
    views, c_specs = [], []
    for (window, dil), c in zip(DIL_GROUPS, caches):
        assert c.shape[1] == window, "window buffers must hold a full window"
        wk = window // dil
        if dil == 1:
            views.append(c)
            c_specs.append(pl.BlockSpec((per_blk, wk, 2, h_g, LANES), lambda s: (s, 0, 0, 0, 0)))
        else:
            assert td <= dil
            views.append(c.reshape(dec_batch, wk, dil, 2, h_g, LANES))
            c_specs.append(pl.BlockSpec((per_blk, wk, td, 2, h_g, LANES), lambda s: (s, 0, 0, 0, 0, 0)))
    kern = functools.partial(_attn_sample_kernel, td=td, h_g=h_g)
    return pl.pallas_call(
        kern,
        out_shape=jax.ShapeDtypeStruct((dec_batch * td, h_g * LANES), F32),
        grid=(dec_batch // per_blk,),
        in_specs=[pl.BlockSpec((rows_blk, parts, h_g, LANES), lambda s: (s, 0, 0, 0))] + c_specs,
        out_specs=pl.BlockSpec((rows_blk, h_g * LANES), lambda s: (s, 0)),
        compiler_params=_cparams(("arbitrary",)),
        name="attn_sample",
    )(uq, *views)


def _merge_kernel(ya_ref, yb_ref, wa_ref, wb_ref, ga_ref, gb_ref, o_ref, *, nc, blocked_gates):
    pa = jnp.dot(ya_ref[...].astype(BF16), wa_ref[...], preferred_element_type=F32)
    pb = jnp.dot(yb_ref[...].astype(BF16), wb_ref[...], preferred_element_type=F32)
    for c in range(nc):
        ls = slice(c * LANES, (c + 1) * LANES)
        ga = ga_ref[c] if blocked_gates else ga_ref[:, ls]
        gb = gb_ref[c] if blocked_gates else gb_ref[:, ls]
        o_ref[:, ls] = (_sigmoid(ga) * pa[:, ls] + _sigmoid(gb) * pb[:, ls]).astype(o_ref.dtype)


def _merge(ya, yb, wa_bf, wb_bf, gates, *, ga_off, gb_off, blocked_gates):
    m = ya.shape[0]
    d = wa_bf.shape[1]
    tm = ROW_TILE
    tn = _largest_divisor(d, (1024, 512, 256, 128))
    nc = tn // LANES
    assert (ga_off * LANES) % tn == 0 and (gb_off * LANES) % tn == 0
    ga_blk, gb_blk = ga_off * LANES // tn, gb_off * LANES // tn
    if blocked_gates:
        ga_spec = pl.BlockSpec((nc, tm, LANES), lambda j, i: (ga_blk + j, i, 0))
        gb_spec = pl.BlockSpec((nc, tm, LANES), lambda j, i: (gb_blk + j, i, 0))
    else:
        ga_spec = pl.BlockSpec((tm, tn), lambda j, i: (i, ga_blk + j))
        gb_spec = pl.BlockSpec((tm, tn), lambda j, i: (i, gb_blk + j))
    kern = functools.partial(_merge_kernel, nc=nc, blocked_gates=blocked_gates)
    return pl.pallas_call(
        kern,
        out_shape=jax.ShapeDtypeStruct((m, d), BF16),
        grid=(d // tn, m // tm),
        in_specs=[pl.BlockSpec((tm, ya.shape[1]), lambda j, i: (i, 0)),
                  pl.BlockSpec((tm, yb.shape[1]), lambda j, i: (i, 0)),
                  pl.BlockSpec((wa_bf.shape[0], tn), lambda j, i: (0, j)),
                  pl.BlockSpec((wb_bf.shape[0], tn), lambda j, i: (0, j)),
                  ga_spec, gb_spec],
        out_specs=pl.BlockSpec((tm, tn), lambda j, i: (i, j)),
        compiler_params=_cparams(("parallel", "parallel")),
        name="merge_blocked" if blocked_gates else "merge_rows",
    )(ya, yb, wa_bf, wb_bf, gates, gates)


def _outproj_kernel(mg_ref, w_ref, x_ref, fw_ref, y_ref, *, tn):
    d = y_ref.shape[1]
    mg = mg_ref[...]
    for c in range(d // tn):
        cols = slice(c * tn, (c + 1) * tn)
        y_ref[:, cols] = x_ref[:, cols] + jnp.dot(mg, w_ref[:, cols], preferred_element_type=F32)

    rc = 64
    fw = fw_ref[...]

    def norm_rows(c, carry):
        rows = pl.ds(pl.multiple_of(c * rc, rc), rc)
        full = y_ref[rows, :]
        ms = jnp.mean(full * full, axis=-1, keepdims=True)
        y_ref[rows, :] = full * lax.rsqrt(ms + RMS_EPS) * fw
        return carry

    lax.fori_loop(0, y_ref.shape[0] // rc, norm_rows, 0)


def _outproj(merged, w_bf, x, fw):
    m, d = x.shape
    tm = ROW_TILE // 2
    tn = _largest_divisor(d, (1024, 512, 256, 128))
    kern = functools.partial(_outproj_kernel, tn=tn)
    return pl.pallas_call(
        kern,
        out_shape=jax.ShapeDtypeStruct((m, d), F32),
        grid=(m // tm,),
        in_specs=[pl.BlockSpec((tm, d), lambda i: (i, 0)),
                  pl.BlockSpec((d, d), lambda i: (0, 0), pipeline_mode=pl.Buffered(1)),
                  pl.BlockSpec((tm, d), lambda i: (i, 0)),
                  pl.BlockSpec((1, d), lambda i: (0, 0))],
        out_specs=pl.BlockSpec((tm, d), lambda i: (i, 0)),
        compiler_params=_cparams(("arbitrary",)),
        name="outproj_norm",
    )(merged, w_bf, x, fw.reshape(1, d))


def kernel(x_prompt, x_sample, cache_kv_w128, cache_kv_w512, cache_kv_w2048, state_hgrn,
           norm_w, w_in, lb_logits, hgrn_norm_w, w_proj_a, w_proj_b, w_out, final_norm_w):
    depth = norm_w.shape[0]
    assert depth == 1, "single-layer step"
    layer = 0
    batch, seq, d = x_prompt.shape
    dec_batch, td, _ = x_sample.shape
    w_a = w_proj_a.shape[1]
    w_bo = w_proj_b.shape[1]
    n_heads = w_a // LANES
    h_g = w_bo // LANES
    n_b = N_GROUPS * h_g
    n_d = d // LANES
    qb_off = 4 * n_heads
    kb_off, vb_off = qb_off + n_b, qb_off + 2 * n_b
    zb_off = qb_off + 3 * n_b
    ga_off = zb_off + h_g
    gb_off = ga_off + n_d
    assert (gb_off + n_d) * LANES == w_in.shape[2]

    xp = x_prompt.reshape(batch * seq, d)
    xs = x_sample.reshape(dec_batch * td, d)
    caches = (cache_kv_w128[layer], cache_kv_w512[layer], cache_kv_w2048[layer])

    wa_bf = _cast_bf16(w_proj_a[layer])
    wb_bf = _cast_bf16(w_proj_b[layer])
    wo_bf = _cast_bf16(w_out[layer])

    h_p = _norm_cast(xp, norm_w[layer])
    h_s = _norm_cast(xs, norm_w[layer])
    u3, us, uq, *kv_p = _inproj(h_p, h_s, w_in[layer], batch=batch, seq=seq, h_g=h_g,
                                attn_off=qb_off, attn_parts=3 * N_GROUPS + 1)
    kv_p = [kv.reshape(1, batch, kv.shape[0] // batch, 2, h_g, LANES) for kv in kv_p]

    nw = hgrn_norm_w[layer].reshape(1, LANES)
    ya_p, s_p = _hgrn_prompt(u3, lb_logits, nw, batch=batch, seq=seq, n_heads=n_heads, layer=layer)
    ya_s, s_s = _hgrn_sample(us, state_hgrn[layer], lb_logits, nw,
                             dec_batch=dec_batch, td=td, n_heads=n_heads, layer=layer)

    yb_p = _attn_prompt(u3, batch=batch, seq=seq, h_g=h_g,
                        q_off=qb_off, k_off=kb_off, v_off=vb_off, z_off=zb_off)
    yb_s = _attn_sample(uq, caches, dec_batch=dec_batch, td=td, h_g=h_g)

    mg_p = _merge(ya_p, yb_p, wa_bf, wb_bf, u3, ga_off=ga_off, gb_off=gb_off, blocked_gates=True)
    mg_s = _merge(ya_s, yb_s, wa_bf, wb_bf, us, ga_off=ga_off, gb_off=gb_off, blocked_gates=False)
    y_p = _outproj(mg_p, wo_bf, xp, final_norm_w)
    y_s = _outproj(mg_s, wo_bf, xs, final_norm_w)

    kv_s = [jnp.stack([uq[:, N_GROUPS + g], uq[:, 2 * N_GROUPS + g]], axis=1)
            .reshape(1, dec_batch, td, 2, h_g, LANES) for g in range(N_GROUPS)]

    return (y_p.reshape(batch, seq, d), y_s.reshape(dec_batch, td, d),
            kv_p[0], kv_p[1], kv_p[2], s_p[None],
            kv_s[0], kv_s[1], kv_s[2], s_s[None])
```

```python
import functools

import jax
import jax.numpy as jnp
from jax import lax
from jax.experimental import pallas as pl
from jax.experimental.pallas import tpu as pltpu

F32 = jnp.float32
BF16 = jnp.bfloat16

LANES = 128
RMS_EPS = 1e-6
LOG2_E = 1.4426950408889634
HGRN_CHUNK = 64
DIL_GROUPS = ((128, 1), (512, 4), (2048, 16))
N_GROUPS = len(DIL_GROUPS)
ROW_TILE = 512
VMEM_LIMIT = 56 * 1024 * 1024


def _cparams(semantics, vmem=VMEM_LIMIT):
    return pltpu.CompilerParams(dimension_semantics=semantics, vmem_limit_bytes=vmem)


def _sigmoid(x):
    return 1.0 / (1.0 + jnp.exp(-x))


def _silu(x):
    return x * _sigmoid(x)


def _largest_divisor(n, candidates):
    for c in candidates:
        if n % c == 0:
            return c
    raise ValueError(f"no tile in {candidates} divides {n}")


def _norm_cast_kernel(x_ref, w_ref, o_ref):
    x = x_ref[...]
    ms = jnp.mean(x * x, axis=-1, keepdims=True)
    o_ref[...] = (x * lax.rsqrt(ms + RMS_EPS) * w_ref[...]).astype(o_ref.dtype)


def _norm_cast(x, w):
    m, d = x.shape
    tm = _largest_divisor(m, (ROW_TILE, 256, 128))
    return pl.pallas_call(
        _norm_cast_kernel,
        out_shape=jax.ShapeDtypeStruct((m, d), BF16),
        grid=(m // tm,),
        in_specs=[pl.BlockSpec((tm, d), lambda i: (i, 0)),
                  pl.BlockSpec((1, d), lambda i: (0, 0))],
        out_specs=pl.BlockSpec((tm, d), lambda i: (i, 0)),
        compiler_params=_cparams(("parallel",)),
        name="norm_cast",
    )(x, w.reshape(1, d))


def _cast_kernel(x_ref, o_ref):
    o_ref[...] = x_ref[...].astype(o_ref.dtype)


def _cast_bf16(w):
    r, c = w.shape
    tr = _largest_divisor(r, (512, 256, 128))
    tc = _largest_divisor(c, (2048, 1024, 512, 256, 128))
    return pl.pallas_call(
        _cast_kernel,
        out_shape=jax.ShapeDtypeStruct((r, c), BF16),
        grid=(r // tr, c // tc),
        in_specs=[pl.BlockSpec((tr, tc), lambda i, j: (i, j))],
        out_specs=pl.BlockSpec((tr, tc), lambda i, j: (i, j)),
        compiler_params=_cparams(("parallel", "parallel")),
        name="cast_bf16",
    )(w)


def _inproj_kernel(hp_ref, hs_ref, w_hbm, u3_ref, us_ref, uq_ref, *rest,
                   n_p, nc, h_g, j_lo, j_hi, kv_plan, tiles_per_seq, n_chunks):
    kv_refs = rest[:len(kv_plan)]
    wbf_ref, stage_ref, sem = rest[len(kv_plan):]
    j = pl.program_id(0)
    i = pl.program_id(1)
    _, kc, tn = stage_ref.shape
    cur = j % 2
    nxt = 1 - cur

    def chunk_copy(tile, c, slot):
        return pltpu.make_async_copy(
            w_hbm.at[pl.ds(pl.multiple_of(c * kc, kc), kc), pl.ds(pl.multiple_of(tile * tn, tn), tn)],
            stage_ref.at[slot], sem.at[slot])

    def cast_chunk(c, slot, buf):
        wbf_ref[buf, pl.ds(pl.multiple_of(c * kc, kc), kc), :] = stage_ref[slot].astype(BF16)

    @pl.when((j == 0) & (i == 0))
    def _():
        def first_tile(c, carry):
            cp = chunk_copy(0, c, c % 2)
            cp.start()
            cp.wait()
            cast_chunk(c, c % 2, 0)
            return carry
        lax.fori_loop(0, n_chunks, first_tile, 0)

    has_next = j + 1 < pl.num_programs(0)

    @pl.when(has_next & (i >= 1) & (i <= n_chunks))
    def _():
        chunk_copy(j + 1, i - 1, (i + 1) % 2).wait()

    @pl.when(has_next & (i < n_chunks))
    def _():
        chunk_copy(j + 1, i, i % 2).start()

    def cast_arrived_chunk():
        cast_chunk(jnp.clip(i - 1, 0, n_chunks - 1), (i + 1) % 2, nxt)

    @pl.when(i < n_p)
    def _():
        cast_arrived_chunk()
        acc = jnp.dot(hp_ref[...], wbf_ref[cur], preferred_element_type=F32)
        for c in range(nc):
            u3_ref[c] = acc[:, c * LANES:(c + 1) * LANES]

        tile_in_seq = i % tiles_per_seq
        for kv_ref, (jk, ck, jv, cv, n_keep_tiles) in zip(kv_refs, kv_plan):
            keep_rows = kv_ref.shape[0]
            for jw, c0 in ((jk, ck), (jv, cv)):
                @pl.when((j == jw) & (tile_in_seq >= tiles_per_seq - n_keep_tiles))
                def _(kv_ref=kv_ref, c0=c0, keep_rows=keep_rows):
                    r0 = acc.shape[0] - keep_rows
                    flat = kv_ref.reshape(keep_rows * h_g, LANES)
                    for h in range(h_g):
                        flat[pl.ds(h, keep_rows, stride=h_g), :] = (
                            acc[r0:, (c0 + h) * LANES:(c0 + h + 1) * LANES])

    @pl.when(i >= n_p)
    def _():
        cast_arrived_chunk()
        acc = jnp.dot(hs_ref[...], wbf_ref[cur], preferred_element_type=F32)
        us_ref[...] = acc

        @pl.when((j >= j_lo) & (j < j_hi))
        def _():
            rows, parts = uq_ref.shape[0], uq_ref.shape[1]
            flat = uq_ref.reshape(rows * parts * h_g, LANES)
            for c in range(nc):
                flat[pl.ds(c, rows, stride=parts * h_g), :] = acc[:, c * LANES:(c + 1) * LANES]


def _inproj(h_p, h_s, w_f32, *, batch, seq, h_g, attn_off, attn_parts):
    mp, d = h_p.shape
    ms = h_s.shape[0]
    n = w_f32.shape[1]
    tm = ROW_TILE
    tn = _largest_divisor(n, (1024, 512, 256, 128))
    n_p, n_s = mp // tm, ms // tm
    n_chunks = n_p + n_s - 1
    assert d % n_chunks == 0 and (d // n_chunks) % 16 == 0
    kc = d // n_chunks
    nc = tn // LANES
    assert nc % h_g == 0 and (attn_off * LANES) % tn == 0 and (attn_parts * h_g) % nc == 0
    ppt = nc // h_g
    j_lo = attn_off * LANES // tn
    n_aj = attn_parts // ppt
    assert seq % tm == 0
    tiles_per_seq = seq // tm

    kv_plan, kv_shapes, kv_specs = [], [], []
    for g, (window, _) in enumerate(DIL_GROUPS):
        keep = min(window, seq)
        blk_rows = min(keep, tm)
        assert keep % blk_rows == 0 and seq % keep == 0
        nkt = keep // blk_rows
        k_part, v_part = N_GROUPS + g, 2 * N_GROUPS + g
        jk, ck = j_lo + k_part // ppt, (k_part % ppt) * h_g
        jv, cv = j_lo + v_part // ppt, (v_part % ppt) * h_g
        assert jk < jv
        kv_plan.append((jk, ck, jv, cv, nkt))
        kv_shapes.append(jax.ShapeDtypeStruct((batch * keep, 2, h_g, LANES), F32))
        last = batch * nkt - 1

        def kv_index(j, i, jk=jk, jv=jv, nkt=nkt, last=last):
            ip = jnp.minimum(i, n_p - 1)
            moving = (ip // tiles_per_seq) * nkt + jnp.clip(ip % tiles_per_seq - (tiles_per_seq - nkt), 0, nkt - 1)
            rb = jnp.where(j < jk, 0, jnp.where((j == jk) | (j == jv), moving, last))
            return (rb, (j >= jv).astype(jnp.int32), 0, 0)

        kv_specs.append(pl.BlockSpec((blk_rows, 1, h_g, LANES), kv_index))

    kern = functools.partial(_inproj_kernel, n_p=n_p, nc=nc, h_g=h_g, j_lo=j_lo, j_hi=j_lo + n_aj,
                             kv_plan=tuple(kv_plan), tiles_per_seq=tiles_per_seq, n_chunks=n_chunks)
    return pl.pallas_call(
        kern,
        out_shape=(jax.ShapeDtypeStruct((n // LANES, mp, LANES), F32),
                   jax.ShapeDtypeStruct((ms, n), F32),
                   jax.ShapeDtypeStruct((ms, attn_parts, h_g, LANES), F32),
                   *kv_shapes),
        grid=(n // tn, n_p + n_s),
        in_specs=[
            pl.BlockSpec((tm, d), lambda j, i: (jnp.minimum(i, n_p - 1), 0)),
            pl.BlockSpec((tm, d), lambda j, i: (jnp.clip(i - n_p, 0, n_s - 1), 0),
                         pipeline_mode=pl.Buffered(1) if n_s == 1 else None),
            pl.BlockSpec(memory_space=pl.ANY),
        ],
        out_specs=(
            pl.BlockSpec((nc, tm, LANES), lambda j, i: (j, jnp.minimum(i, n_p - 1), 0)),
            pl.BlockSpec((tm, tn), lambda j, i: (jnp.clip(i - n_p, 0, n_s - 1), j)),
            pl.BlockSpec((tm, ppt, h_g, LANES),
                         lambda j, i: (jnp.clip(i - n_p, 0, n_s - 1), jnp.clip(j - j_lo, 0, n_aj - 1), 0, 0)),
            *kv_specs,
        ),
        scratch_shapes=[pltpu.VMEM((2, d, tn), BF16),
                        pltpu.VMEM((2, kc, tn), F32),
                        pltpu.SemaphoreType.DMA((2,))],
        compiler_params=_cparams(("arbitrary", "arbitrary")),
        name="inproj",
    )(h_p, h_s, w_f32)


def _lower_bound(lb_logits, layer):
    mx = jnp.max(lb_logits, axis=0, keepdims=True)
    e = jnp.exp(lb_logits - mx)
    den = jnp.sum(e, axis=0, keepdims=True)
    num = e[0:1]
    for r in range(1, layer + 1):
        num = num + e[r:r + 1]
    return num / den


def _split3_bf16(x):
    hi = x.astype(BF16)
    r1 = x - hi.astype(F32)
    mid = r1.astype(BF16)
    lo = (r1 - mid.astype(F32)).astype(BF16)
    return hi, mid, lo


def _hgrn_prompt_kernel(q_ref, f_ref, i_ref, z_ref, lb_ref, nw_ref, y_ref, sfin_ref, st_ref,
                        *, hb, tb, chunk, cpb, layer):
    t = pl.program_id(2)

    @pl.when(t == 0)
    def _():
        st_ref[...] = jnp.zeros_like(st_ref)

    lb_all = _lower_bound(lb_ref[...], layer)
    nw = nw_ref[...]
    rb = cpb * chunk
    row = lax.broadcasted_iota(jnp.int32, (rb, rb), 0)
    col = lax.broadcasted_iota(jnp.int32, (rb, rb), 1)
    causal = (row >= col) & (row // chunk == col // chunk)
    tril = jnp.where(causal, 1.0, 0.0).astype(BF16)
    dk = q_ref.shape[-1]
    qscale = float(dk) ** -0.5
    contract_last = (((1,), (1,)), ((), ()))
    contract_first = (((0,), (0,)), ((), ()))

    def block_body(blk, carry):
        r0 = pl.multiple_of(blk * rb, rb)
        for hl in range(hb):
            lb = lb_all[:, hl * LANES:(hl + 1) * LANES]
            qa = q_ref[hl, pl.ds(r0, rb), :]
            fa = f_ref[hl, pl.ds(r0, rb), :]
            v = i_ref[hl, pl.ds(r0, rb), :]
            za = z_ref[hl, pl.ds(r0, rb), :]
            q = _silu(qa) * qscale
            f = lb + (1.0 - lb) * _sigmoid(fa)
            log_f = jnp.log(f)
            k = 1.0 - f
            hi, mid, lo = _split3_bf16(log_f)
            parts = jnp.dot(tril, jnp.concatenate([hi, mid, lo], axis=1),
                            preferred_element_type=F32)
            g = (parts[:, 0:LANES] + parts[:, LANES:2 * LANES]) + parts[:, 2 * LANES:3 * LANES]
            g_last = [g[(c + 1) * chunk - 1:(c + 1) * chunk, :] for c in range(cpb)]
            g_last_rows = jnp.concatenate(
                [jnp.broadcast_to(gl, (chunk, LANES)) for gl in g_last], axis=0)
            q_dec = (q * jnp.exp(g)).astype(BF16)
            k_rel = (k * jnp.exp(-g)).astype(BF16)
            k_end = (k * jnp.exp(g_last_rows - g)).astype(BF16)
            vb = v.astype(BF16)
            a = lax.dot_general(q_dec, k_rel, contract_last, preferred_element_type=F32)
            a = jnp.where(causal, a, 0.0)
            o_intra = jnp.dot(a.astype(BF16), vb, preferred_element_type=F32)
            st = st_ref[hl]
            o_inter = []
            for c in range(cpb):
                rows = slice(c * chunk, (c + 1) * chunk)
                o_inter.append(lax.dot_general(q_dec[rows], st.astype(BF16), contract_last,
                                               preferred_element_type=F32))
                ds_t = lax.dot_general(vb[rows], k_end[rows], contract_first,
                                       preferred_element_type=F32)
                st = jnp.exp(g_last[c]) * st + ds_t
            st_ref[hl] = st
            o = o_intra + jnp.concatenate(o_inter, axis=0)
            ms = jnp.mean(o * o, axis=-1, keepdims=True)
            y = (o * lax.rsqrt(ms + RMS_EPS) * nw) * _silu(za)
            y_ref[pl.ds(r0, rb), hl * LANES:(hl + 1) * LANES] = y.astype(y_ref.dtype)
        return carry

    lax.fori_loop(0, tb // rb, block_body, 0)

    @pl.when(t == pl.num_programs(2) - 1)
    def _():
        for hl in range(hb):
            sfin_ref[0, hl] = st_ref[hl].T


def _hgrn_prompt(u3, lb_logits, nw, *, batch, seq, n_heads, layer):
    hb = _largest_divisor(n_heads, (8, 4, 2, 1))
    tb = _largest_divisor(seq, (1024, 512, 256, 128, 64))
    chunk = HGRN_CHUNK
    assert tb % chunk == 0
    nt = seq // tb
    nhb = n_heads // hb
    cpb = _largest_divisor(tb // chunk, (4, 2, 1))
    kern = functools.partial(_hgrn_prompt_kernel, hb=hb, tb=tb, chunk=chunk, cpb=cpb, layer=layer)

    def in_spec(part):
        return pl.BlockSpec((hb, tb, LANES), lambda b, h, t: (part * nhb + h, b * nt + t, 0))

    return pl.pallas_call(
        kern,
        out_shape=(jax.ShapeDtypeStruct((batch * seq, n_heads * LANES), BF16),
                   jax.ShapeDtypeStruct((batch, n_heads, LANES, LANES), F32)),
        grid=(batch, nhb, nt),
        in_specs=[in_spec(0), in_spec(1), in_spec(2), in_spec(3),
                  pl.BlockSpec((lb_logits.shape[0], hb * LANES), lambda b, h, t: (0, h)),
                  pl.BlockSpec((1, LANES), lambda b, h, t: (0, 0))],
        out_specs=(pl.BlockSpec((tb, hb * LANES), lambda b, h, t: (b * nt + t, h)),
                   pl.BlockSpec((1, hb, LANES, LANES), lambda b, h, t: (b, h, 0, 0))),
        scratch_shapes=[pltpu.VMEM((hb, LANES, LANES), F32)],
        compiler_params=_cparams(("parallel", "parallel", "arbitrary")),
        name="hgrn_prompt",
    )(u3, u3, u3, u3, lb_logits, nw)


def _hgrn_sample_kernel(q_ref, f_ref, i_ref, z_ref, lb_ref, nw_ref, s_ref, y_ref, so_ref, o_scr,
                        *, bb, td, layer):
    lb = _lower_bound(lb_ref[...], layer)
    dk = q_ref.shape[-1]
    q = _silu(q_ref[...]) * (float(dk) ** -0.5)
    f = lb + (1.0 - lb) * _sigmoid(f_ref[...])
    k = 1.0 - f
    v = i_ref[...]
    rows = bb * td
    sub = 8
    contract_last = (((1,), (1,)), ((), ()))
    contract_first = (((0,), (0,)), ((), ()))
    t_idx = lax.broadcasted_iota(jnp.int32, (rows, LANES), 0) % td
    p_inc = f
    s = 1
    while s < td:
        p_inc = p_inc * jnp.where(t_idx >= s, pltpu.roll(p_inc, s, axis=0), 1.0)
        s *= 2
    e_suf = jnp.where(t_idx + 1 < td, pltpu.roll(f, rows - 1, axis=0), 1.0)
    s = 1
    while s < td:
        e_suf = e_suf * jnp.where(t_idx + s < td, pltpu.roll(e_suf, rows - s, axis=0), 1.0)
        s *= 2
    q_dec = q * p_inc
    k_end = k * e_suf
    a = lax.dot_general(q_dec.astype(BF16), (k / p_inc).astype(BF16), contract_last,
                        preferred_element_type=F32)
    ri = lax.broadcasted_iota(jnp.int32, (rows, rows), 0)
    ci = lax.broadcasted_iota(jnp.int32, (rows, rows), 1)
    a = jnp.where((ri // td == ci // td) & (ci <= ri), a, 0.0)
    o_intra = jnp.dot(a.astype(BF16), v.astype(BF16), preferred_element_type=F32)
    dec_t = p_inc.T
    row8 = lax.broadcasted_iota(jnp.int32, (sub, LANES), 0)
    for grp in range(rows // sub):
        sl = slice(grp * sub, (grp + 1) * sub)
        qd8 = q_dec[sl].astype(BF16)
        o8 = jnp.zeros((sub, LANES), F32)
        for u in range(sub // td):
            bl = grp * (sub // td) + u
            mine = (row8 >= u * td) & (row8 < (u + 1) * td)
            s0 = s_ref[bl, 0]
            o8 = jnp.where(mine, jnp.dot(qd8, s0.astype(BF16), preferred_element_type=F32), o8)
            ds = lax.dot_general(jnp.where(mine, k_end[sl], 0.0).astype(BF16),
                                 jnp.where(mine, v[sl], 0.0).astype(BF16), contract_first,
                                 preferred_element_type=F32)
            j = bl * td + td - 1
            so_ref[bl, 0] = dec_t[:, j:j + 1] * s0 + ds
        o_scr[sl, :] = o8
    o = o_intra + o_scr[...]
    ms = jnp.mean(o * o, axis=-1, keepdims=True)
    y = (o * lax.rsqrt(ms + RMS_EPS) * nw_ref[...]) * _silu(z_ref[...])
    y_ref[...] = y.astype(y_ref.dtype)


def _hgrn_sample(us, state, lb_logits, nw, *, dec_batch, td, n_heads, layer):
    rows = LANES
    assert rows % td == 0 and 8 % td == 0
    bb = rows // td
    assert dec_batch % bb == 0
    kern = functools.partial(_hgrn_sample_kernel, bb=bb, td=td, layer=layer)

    def in_spec(part):
        return pl.BlockSpec((rows, LANES), lambda bi, h: (bi, part * n_heads + h))

    return pl.pallas_call(
        kern,
        out_shape=(jax.ShapeDtypeStruct((dec_batch * td, n_heads * LANES), BF16),
                   jax.ShapeDtypeStruct(state.shape, F32)),
        grid=(dec_batch // bb, n_heads),
        in_specs=[in_spec(0), in_spec(1), in_spec(2), in_spec(3),
                  pl.BlockSpec((lb_logits.shape[0], LANES), lambda bi, h: (0, h)),
                  pl.BlockSpec((1, LANES), lambda bi, h: (0, 0)),
                  pl.BlockSpec((bb, 1, LANES, LANES), lambda bi, h: (bi, h, 0, 0))],
        out_specs=(pl.BlockSpec((rows, LANES), lambda bi, h: (bi, h)),
                   pl.BlockSpec((bb, 1, LANES, LANES), lambda bi, h: (bi, h, 0, 0))),
        scratch_shapes=[pltpu.VMEM((rows, LANES), F32)],
        compiler_params=_cparams(("parallel", "parallel")),
        name="hgrn_sample",
    )(us, us, us, us, lb_logits, nw, state)


def _attn_prompt_kernel(q_ref, k_ref, v_ref, z_ref, y_ref, m_acc, den_acc, o_acc, *, seq, scale):
    g = pl.program_id(2)
    neg_inf = -jnp.inf
    contract_last = (((1,), (1,)), ((), ()))

    def run_group(first, window, dil):
        wk = window // dil
        nb = seq // window
        ii = lax.broadcasted_iota(jnp.int32, (wk, wk), 0)
        jj = lax.broadcasted_iota(jnp.int32, (wk, wk), 1)
        mask_prev = jj >= ii
        mask_own = jj <= ii
        ones = jnp.ones((wk, LANES), BF16)

        def rows(start):
            return pl.ds(start, wk, stride=dil) if dil > 1 else pl.ds(start, wk)

        def block(idx, carry):
            n, r = idx // dil, idx % dil
            own = rows(n * window + r)
            prev = rows(jnp.maximum(n - 1, 0) * window + r)
            q = (q_ref[0, own, :] * scale).astype(BF16)
            k_own = k_ref[0, own, :].astype(BF16)
            k_prev = k_ref[0, prev, :].astype(BF16)
            v_own = jnp.concatenate([v_ref[0, own, :].astype(BF16), ones], axis=1)
            v_prev = jnp.concatenate([v_ref[0, prev, :].astype(BF16), ones], axis=1)
            s_own = lax.dot_general(q, k_own, contract_last, preferred_element_type=F32)
            s_prev = lax.dot_general(q, k_prev, contract_last, preferred_element_type=F32)
            no_prev = jnp.where(n > 0, 0.0, neg_inf)
            s_own = jnp.where(mask_own, s_own, neg_inf)
            s_prev = jnp.where(mask_prev, s_prev + no_prev, neg_inf)
            m = jnp.max(jnp.maximum(s_own, s_prev), axis=-1, keepdims=True)
            p_own = jnp.exp2(s_own - m).astype(BF16)
            p_prev = jnp.exp2(s_prev - m).astype(BF16)
            acc = (jnp.dot(p_own, v_own, preferred_element_type=F32)
                   + jnp.dot(p_prev, v_prev, preferred_element_type=F32))
            o, den = acc[:, :LANES], acc[:, LANES:]
            if first:
                m_acc[own, :] = jnp.broadcast_to(m, (wk, LANES))
                den_acc[own, :] = den
                o_acc[own, :] = o
            else:
                m_old = m_acc[own, :]
                m_new = jnp.maximum(m_old, m)
                a_old = jnp.exp2(m_old - m_new)
                a_new = jnp.exp2(m - m_new)
                m_acc[own, :] = m_new
                den_acc[own, :] = den_acc[own, :] * a_old + den * a_new
                o_acc[own, :] = o_acc[own, :] * a_old + o * a_new
            return carry

        lax.fori_loop(0, nb * dil, block, 0, unroll=16)

    for gi, (window, dil) in enumerate(DIL_GROUPS):
        @pl.when(g == N_GROUPS - 1 - gi)
        def _(gi=gi, window=window, dil=dil):
            run_group(gi == N_GROUPS - 1, window, dil)

    @pl.when(g == N_GROUPS - 1)
    def _():
        y = (o_acc[...] / den_acc[...]) * _silu(z_ref[0])
        y_ref[...] = y.astype(y_ref.dtype)


def _attn_prompt(u3, *, batch, seq, h_g, q_off, k_off, v_off, z_off):
    for window, _ in DIL_GROUPS:
        assert seq % window == 0
    kern = functools.partial(_attn_prompt_kernel, seq=seq,
                             scale=float(LANES) ** -0.5 * LOG2_E)

    def in_spec(off):
        return pl.BlockSpec((1, seq, LANES),
                            lambda b, h, g: (off + (N_GROUPS - 1 - g) * h_g + h, b, 0))

    return pl.pallas_call(
        kern,
        out_shape=jax.ShapeDtypeStruct((batch * seq, h_g * LANES), BF16),
        grid=(batch, h_g, N_GROUPS),
        in_specs=[in_spec(q_off), in_spec(k_off), in_spec(v_off),
                  pl.BlockSpec((1, seq, LANES), lambda b, h, g: (z_off + h, b, 0))],
        out_specs=pl.BlockSpec((seq, LANES), lambda b, h, g: (b, h)),
        scratch_shapes=[pltpu.VMEM((seq, LANES), F32)] * 3,
        compiler_params=_cparams(("parallel", "parallel", "arbitrary")),
        name="attn_prompt",
    )(u3, u3, u3, u3)


def _attn_sample_kernel(uq_ref, c0_ref, c1_ref, c2_ref, y_ref, *, td, h_g):
    def element(e, carry):
        _attn_sample_element(uq_ref, (c0_ref, c1_ref, c2_ref), y_ref, e, td=td, h_g=h_g)
        return carry

    lax.fori_loop(0, uq_ref.shape[0] // td, element, 0)


def _attn_sample_element(uq_ref, c_refs, y_ref, e, *, td, h_g):
    ro = e * td
    qscale = float(LANES) ** -0.5 * LOG2_E
    neg_inf = -jnp.inf
    contract_last = (((1,), (1,)), ((), ()))
    q_part, k_part, v_part, z_part = 0, N_GROUPS, 2 * N_GROUPS, 3 * N_GROUPS
    wk = DIL_GROUPS[0][0] // DIL_GROUPS[0][1]
    nl = N_GROUPS * h_g
    shape3 = (wk, h_g, nl)
    c_row = lax.broadcasted_iota(jnp.int32, shape3, 0)
    key_head = lax.broadcasted_iota(jnp.int32, shape3, 1)
    lane = lax.broadcasted_iota(jnp.int32, shape3, 2)
    head_match = lane % h_g == key_head
    e_row = lax.broadcasted_iota(jnp.int32, (nl, LANES), 0)
    expand = [jnp.where(e_row // h_g == g, 1.0, 0.0).astype(BF16) for g in range(N_GROUPS)]
    zeros_q = jnp.zeros((h_g, LANES), F32)

    for i in range(td):
        r = ro + i
        valid = head_match
        k_parts, q_rows, v_cs, s_new, v_new = [], [], [], [], []
        for g, (window, dil) in enumerate(DIL_GROUPS):
            assert window // dil == wk
            c_ref = c_refs[g]
            q_t = uq_ref[r, q_part + g]
            if dil == 1:
                k_c, v_c = c_ref[e, :, 0], c_ref[e, :, 1]
                new_js = tuple(range(i + 1))
                valid = valid & ((lane // h_g != g) | (c_row >= i))
            else:
                k_c, v_c = c_ref[e, :, i, 0], c_ref[e, :, i, 1]
                new_js = (i,)
            k_parts.append(k_c.reshape(wk * h_g, LANES).astype(BF16))
            q_rows.append(jnp.concatenate(
                [q_t * qscale if gg == g else zeros_q for gg in range(N_GROUPS)], axis=1))
            v_cs.append(v_c)
            for jn in new_js:
                s_new.append(jnp.sum(uq_ref[ro + jn, k_part + g] * q_t, axis=-1, keepdims=True) * qscale)
                v_new.append(uq_ref[ro + jn, v_part + g])
        s = lax.dot_general(jnp.concatenate(k_parts, axis=1),
                            jnp.concatenate(q_rows, axis=0).astype(BF16),
                            contract_last, preferred_element_type=F32)
        s = jnp.where(valid, s.reshape(shape3), neg_inf)
        m = jnp.max(jnp.max(s, axis=0), axis=-1, keepdims=True)
        for sn in s_new:
            m = jnp.maximum(m, sn)
        p = jnp.exp2(s - m[None])
        den = jnp.sum(jnp.sum(p, axis=0), axis=-1, keepdims=True)
        p_bf = p.astype(BF16).reshape(wk * h_g, nl)
        y = jnp.zeros((h_g, LANES), F32)
        for g in range(N_GROUPS):
            pb = jnp.dot(p_bf, expand[g], preferred_element_type=F32)
            y = y + jnp.sum(pb.reshape(wk, h_g, LANES) * v_cs[g], axis=0)
        for sn, vn in zip(s_new, v_new):
            pn = jnp.exp2(sn - m)
            den = den + pn
            y = y + pn * vn
        y = (y / den) * _silu(uq_ref[r, z_part])
        y_ref[pl.ds(r, 1), :] = jnp.concatenate([y[h:h + 1, :] for h in range(h_g)], axis=1)


def _attn_sample(uq, caches, *, dec_batch, td, h_g):
    rows_blk = 8
    assert rows_blk % td == 0 and (dec_batch * td) % rows_blk == 0
    per_blk = rows_blk // td
    parts = uq.shape[1]
    views, c_specs = [], []
    for (window, dil), c in zip(DIL_GROUPS, caches):
        assert c.shape[1] == window, "window buffers must hold a full window"
        wk = window // dil
        if dil == 1:
            views.append(c)
            c_specs.append(pl.BlockSpec((per_blk, wk, 2, h_g, LANES), lambda s: (s, 0, 0, 0, 0)))
        else:
            assert td <= dil
            views.append(c.reshape(dec_batch, wk, dil, 2, h_g, LANES))
            c_specs.append(pl.BlockSpec((per_blk, wk, td, 2, h_g, LANES), lambda s: (s, 0, 0, 0, 0, 0)))
    kern = functools.partial(_attn_sample_kernel, td=td, h_g=h_g)
    return pl.pallas_call(
        kern,
        out_shape=jax.ShapeDtypeStruct((dec_batch * td, h_g * LANES), F32),
        grid=(dec_batch // per_blk,),
        in_specs=[pl.BlockSpec((rows_blk, parts, h_g, LANES), lambda s: (s, 0, 0, 0))] + c_specs,
        out_specs=pl.BlockSpec((rows_blk, h_g * LANES), lambda s: (s, 0)),
        compiler_params=_cparams(("arbitrary",)),
        name="attn_sample",
    )(uq, *views)


def _merge_kernel(ya_ref, yb_ref, wa_ref, wb_ref, ga_ref, gb_ref, o_ref, *, nc, blocked_gates):
    pa = jnp.dot(ya_ref[...].astype(BF16), wa_ref[...], preferred_element_type=F32)
    pb = jnp.dot(yb_ref[...].astype(BF16), wb_ref[...], preferred_element_type=F32)
    for c in range(nc):
        ls = slice(c * LANES, (c + 1) * LANES)
        ga = ga_ref[c] if blocked_gates else ga_ref[:, ls]
        gb = gb_ref[c] if blocked_gates else gb_ref[:, ls]
        o_ref[:, ls] = (_sigmoid(ga) * pa[:, ls] + _sigmoid(gb) * pb[:, ls]).astype(o_ref.dtype)


def _merge(ya, yb, wa_bf, wb_bf, gates, *, ga_off, gb_off, blocked_gates):
    m = ya.shape[0]
    d = wa_bf.shape[1]
    tm = ROW_TILE
    tn = _largest_divisor(d, (1024, 512, 256, 128))
    nc = tn // LANES
    assert (ga_off * LANES) % tn == 0 and (gb_off * LANES) % tn == 0
    ga_blk, gb_blk = ga_off * LANES // tn, gb_off * LANES // tn
    if blocked_gates:
        ga_spec = pl.BlockSpec((nc, tm, LANES), lambda j, i: (ga_blk + j, i, 0))
        gb_spec = pl.BlockSpec((nc, tm, LANES), lambda j, i: (gb_blk + j, i, 0))
    else:
        ga_spec = pl.BlockSpec((tm, tn), lambda j, i: (i, ga_blk + j))
        gb_spec = pl.BlockSpec((tm, tn), lambda j, i: (i, gb_blk + j))
    kern = functools.partial(_merge_kernel, nc=nc, blocked_gates=blocked_gates)
    return pl.pallas_call(
        kern,
        out_shape=jax.ShapeDtypeStruct((m, d), BF16),
        grid=(d // tn, m // tm),
        in_specs=[pl.BlockSpec((tm, ya.shape[1]), lambda j, i: (i, 0)),
                  pl.BlockSpec((tm, yb.shape[1]), lambda j, i: (i, 0)),
                  pl.BlockSpec((wa_bf.shape[0], tn), lambda j, i: (0, j)),
                  pl.BlockSpec((wb_bf.shape[0], tn), lambda j, i: (0, j)),
                  ga_spec, gb_spec],
        out_specs=pl.BlockSpec((tm, tn), lambda j, i: (i, j)),
        compiler_params=_cparams(("parallel", "parallel")),
        name="merge_blocked" if blocked_gates else "merge_rows",
    )(ya, yb, wa_bf, wb_bf, gates, gates)


def _outproj_kernel(mg_ref, w_ref, x_ref, fw_ref, y_ref, *, tn):
    d = y_ref.shape[1]
    mg = mg_ref[...]
    for c in range(d // tn):
        cols = slice(c * tn, (c + 1) * tn)
        y_ref[:, cols] = x_ref[:, cols] + jnp.dot(mg, w_ref[:, cols], preferred_element_type=F32)

    rc = 64
    fw = fw_ref[...]

    def norm_rows(c, carry):
        rows = pl.ds(pl.multiple_of(c * rc, rc), rc)
        full = y_ref[rows, :]
        ms = jnp.mean(full * full, axis=-1, keepdims=True)
        y_ref[rows, :] = full * lax.rsqrt(ms + RMS_EPS) * fw
        return carry

    lax.fori_loop(0, y_ref.shape[0] // rc, norm_rows, 0)


def _outproj(merged, w_bf, x, fw):
    m, d = x.shape
    tm = ROW_TILE // 2
    tn = _largest_divisor(d, (1024, 512, 256, 128))
    kern = functools.partial(_outproj_kernel, tn=tn)
    return pl.pallas_call(
        kern,
        out_shape=jax.ShapeDtypeStruct((m, d), F32),
        grid=(m // tm,),
        in_specs=[pl.BlockSpec((tm, d), lambda i: (i, 0)),
                  pl.BlockSpec((d, d), lambda i: (0, 0), pipeline_mode=pl.Buffered(1)),
                  pl.BlockSpec((tm, d), lambda i: (i, 0)),
                  pl.BlockSpec((1, d), lambda i: (0, 0))],
        out_specs=pl.BlockSpec((tm, d), lambda i: (i, 0)),
        compiler_params=_cparams(("arbitrary",)),
        name="outproj_norm",
    )(merged, w_bf, x, fw.reshape(1, d))


def kernel(x_prompt, x_sample, cache_kv_w128, cache_kv_w512, cache_kv_w2048, state_hgrn,
           norm_w, w_in, lb_logits, hgrn_norm_w, w_proj_a, w_proj_b, w_out, final_norm_w):
    depth = norm_w.shape[0]
    assert depth == 1, "single-layer step"
    layer = 0
    batch, seq, d = x_prompt.shape
    dec_batch, td, _ = x_sample.shape
    w_a = w_proj_a.shape[1]
    w_bo = w_proj_b.shape[1]
    n_heads = w_a // LANES
    h_g = w_bo // LANES
    n_b = N_GROUPS * h_g
    n_d = d // LANES
    qb_off = 4 * n_heads
    kb_off, vb_off = qb_off + n_b, qb_off + 2 * n_b
    zb_off = qb_off + 3 * n_b
    ga_off = zb_off + h_g
    gb_off = ga_off + n_d
    assert (gb_off + n_d) * LANES == w_in.shape[2]

    xp = x_prompt.reshape(batch * seq, d)
    xs = x_sample.reshape(dec_batch * td, d)
    caches = (cache_kv_w128[layer], cache_kv_w512[layer], cache_kv_w2048[layer])

    wa_bf = _cast_bf16(w_proj_a[layer])
    wb_bf = _cast_bf16(w_proj_b[layer])
    wo_bf = _cast_bf16(w_out[layer])

    h_p = _norm_cast(xp, norm_w[layer])
    h_s = _norm_cast(xs, norm_w[layer])
    u3, us, uq, *kv_p = _inproj(h_p, h_s, w_in[layer], batch=batch, seq=seq, h_g=h_g,
                                attn_off=qb_off, attn_parts=3 * N_GROUPS + 1)
    kv_p = [kv.reshape(1, batch, kv.shape[0] // batch, 2, h_g, LANES) for kv in kv_p]

    nw = hgrn_norm_w[layer].reshape(1, LANES)
    ya_p, s_p = _hgrn_prompt(u3, lb_logits, nw, batch=batch, seq=seq, n_heads=n_heads, layer=layer)
    ya_s, s_s = _hgrn_sample(us, state_hgrn[layer], lb_logits, nw,
                             dec_batch=dec_batch, td=td, n_heads=n_heads, layer=layer)

    yb_p = _attn_prompt(u3, batch=batch, seq=seq, h_g=h_g,
                        q_off=qb_off, k_off=kb_off, v_off=vb_off, z_off=zb_off)
    yb_s = _attn_sample(uq, caches, dec_batch=dec_batch, td=td, h_g=h_g)

    mg_p = _merge(ya_p, yb_p, wa_bf, wb_bf, u3, ga_off=ga_off, gb_off=gb_off, blocked_gates=True)
    mg_s = _merge(ya_s, yb_s, wa_bf, wb_bf, us, ga_off=ga_off, gb_off=gb_off, blocked_gates=False)
    y_p = _outproj(mg_p, wo_bf, xp, final_norm_w)
    y_s = _outproj(mg_s, wo_bf, xs, final_norm_w)

    kv_s = [jnp.stack([uq[:, N_GROUPS + g], uq[:, 2 * N_GROUPS + g]], axis=1)
            .reshape(1, dec_batch, td, 2, h_g, LANES) for g in range(N_GROUPS)]

    return (y_p.reshape(batch, seq, d), y_s.reshape(dec_batch, td, d),
            kv_p[0], kv_p[1], kv_p[2], s_p[None],
            kv_s[0], kv_s[1], kv_s[2], s_s[None])
```
